```python
import jax, jax.numpy as jnp
from jax import lax
import numpy as np

D_MODEL = 1024
BATCH = 16
SEQ = 2048
DEPTH = 4

N_MIXERS = 2
N_A_LAYERS = (DEPTH + 1) // 2
N_B_LAYERS = DEPTH // 2
CHUNK = 128
A_HEADS = 8
A_WIDTH = D_MODEL
A_HEAD_DIM = A_WIDTH // A_HEADS
POOL_WINDOWS = (2, 4, 8, 16)
B_GROUPS = len(POOL_WINDOWS)
B_WIDTH = D_MODEL
B_GROUP_DIM = B_WIDTH // B_GROUPS
D_FF = 2816
CONV_WIDTH = 3
PLE_DIM = 256
DEEPNORM_ALPHA = (2.0 * DEPTH) ** 0.25
DEEPNORM_BETA = (8.0 * DEPTH) ** -0.25
LN_EPS = 1e-5

kernel_name = "hybrid_gmlp_pool_convffn_encoder"


def layer_norm(x, g, b):
    xf = x.astype(jnp.float32)
    mu = jnp.mean(xf, axis=-1, keepdims=True)
    var = jnp.mean(jnp.square(xf - mu), axis=-1, keepdims=True)
    y = (xf - mu) * lax.rsqrt(var + LN_EPS)
    return (y * g.astype(jnp.float32) + b.astype(jnp.float32)).astype(x.dtype)


def mixer_a(x, w_in, ln_g, ln_b, w_s, b_s, w_o):
    bsz, seq, _ = x.shape
    h = jax.nn.gelu(x @ w_in, approximate=False)
    u, v = jnp.split(h, 2, axis=-1)
    v = layer_norm(v, ln_g, ln_b)
    v = v.reshape(bsz, seq // CHUNK, CHUNK, A_HEADS, A_HEAD_DIM)
    s = jnp.einsum('hpq,bcqhd->bcphd', w_s, v) + b_s.T[:, :, None]
    s = s.reshape(bsz, seq, A_WIDTH)
    return (u * s) @ w_o


def centred_pool_minus_self(h, window):
    seq = h.shape[1]
    hf = h.astype(jnp.float32)
    cs = jnp.pad(jnp.cumsum(hf, axis=1), ((0, 0), (1, 0), (0, 0)))
    t = jnp.arange(seq)
    lo = jnp.maximum(t - window // 2, 0)
    hi = jnp.minimum(t + (window - window // 2 - 1), seq - 1)
    total = jnp.take(cs, hi + 1, axis=1) - jnp.take(cs, lo, axis=1)
    count = (hi - lo + 1).astype(jnp.float32)[None, :, None]
    return (total / count - hf).astype(h.dtype)


def mixer_b(x, w_in, w_grp, scale, w_o):
    bsz, seq, _ = x.shape
    h = (x @ w_in).reshape(bsz, seq, B_GROUPS, B_GROUP_DIM)
    pooled = jnp.stack(
        [centred_pool_minus_self(h[:, :, g], w) for g, w in enumerate(POOL_WINDOWS)], axis=2)
    y = jnp.einsum('bsgc,gcd->bsgd', pooled, w_grp).reshape(bsz, seq, B_WIDTH)
    return (y * scale) @ w_o


def channel_mixer(x, w_up, conv_w, conv_b, w_down):
    h = x @ w_up
    ch = h.shape[-1]
    h = lax.conv_general_dilated(
        h, conv_w[:, None, :], window_strides=(1,),
        padding=((CONV_WIDTH // 2, CONV_WIDTH // 2),),
        dimension_numbers=('NWC', 'WIO', 'NWC'),
        feature_group_count=ch) + conv_b
    gate, val = jnp.split(h, 2, axis=-1)
    return (jax.nn.gelu(gate, approximate=False) * val) @ w_down


def _normal(k, shape, scale):
    return jax.random.normal(k, shape, jnp.float32) * scale


def setup_inputs(seed: int = 0) -> dict:
    key = jax.random.key(seed)
    ks = jax.random.split(key, 24)
    D = D_MODEL
    beta = DEEPNORM_BETA
    return {
        "x": _normal(ks[0], (BATCH, SEQ, D), 1.0),
        "p": _normal(ks[1], (DEPTH, BATCH, SEQ, PLE_DIM), 1.0),
        "a_w_in": _normal(ks[2], (N_A_LAYERS, D, 2 * A_WIDTH), D ** -0.5),
        "a_ln_g": 1.0 + _normal(ks[3], (N_A_LAYERS, A_WIDTH), 0.02),
        "a_ln_b": _normal(ks[4], (N_A_LAYERS, A_WIDTH), 0.02),
        "a_w_s": _normal(ks[5], (N_A_LAYERS, A_HEADS, CHUNK, CHUNK), 0.5 * CHUNK ** -0.5),
        "a_b_s": 1.0 + _normal(ks[6], (N_A_LAYERS, A_HEADS, CHUNK), 0.02),
        "a_w_o": _normal(ks[7], (N_A_LAYERS, A_WIDTH, D), A_WIDTH ** -0.5 * beta),
        "b_w_in": _normal(ks[8], (N_B_LAYERS, D, B_WIDTH), D ** -0.5),
        "b_w_grp": _normal(ks[9], (N_B_LAYERS, B_GROUPS, B_GROUP_DIM, B_GROUP_DIM), B_GROUP_DIM ** -0.5),
        "b_scale": 1.0 + _normal(ks[10], (N_B_LAYERS, B_WIDTH), 0.02),
        "b_w_o": _normal(ks[11], (N_B_LAYERS, B_WIDTH, D), B_WIDTH ** -0.5 * beta),
        "ffn_w_up": _normal(ks[12], (DEPTH, D, 2 * D_FF), D ** -0.5),
        "ffn_conv_w": _normal(ks[13], (DEPTH, CONV_WIDTH, 2 * D_FF), CONV_WIDTH ** -0.5),
        "ffn_conv_b": _normal(ks[14], (DEPTH, 2 * D_FF), 0.02),
        "ffn_w_down": _normal(ks[15], (DEPTH, D_FF, D), D_FF ** -0.5 * beta),
        "ln1_g": 1.0 + _normal(ks[16], (DEPTH, D), 0.02),
        "ln1_b": _normal(ks[17], (DEPTH, D), 0.02),
        "ln2_g": 1.0 + _normal(ks[18], (DEPTH, D), 0.02),
        "ln2_b": _normal(ks[19], (DEPTH, D), 0.02),
        "ple_w_p": _normal(ks[20], (DEPTH, PLE_DIM, D), PLE_DIM ** -0.5),
        "ple_w_gate": _normal(ks[21], (DEPTH, D, D), D ** -0.5),
        "ple_b_gate": _normal(ks[22], (DEPTH, D), 0.02),
    }


def reference(x, p, a_w_in, a_ln_g, a_ln_b, a_w_s, a_b_s, a_w_o,
              b_w_in, b_w_grp, b_scale, b_w_o,
              ffn_w_up, ffn_conv_w, ffn_conv_b, ffn_w_down,
              ln1_g, ln1_b, ln2_g, ln2_b,
              ple_w_p, ple_w_gate, ple_b_gate):
    for i in range(DEPTH):
        j = i // N_MIXERS
        if i % N_MIXERS == 0:
            mixed = mixer_a(x, a_w_in[j], a_ln_g[j], a_ln_b[j], a_w_s[j], a_b_s[j], a_w_o[j])
        else:
            mixed = mixer_b(x, b_w_in[j], b_w_grp[j], b_scale[j], b_w_o[j])
        x = layer_norm(DEEPNORM_ALPHA * x + mixed, ln1_g[i], ln1_b[i])
        ffn = channel_mixer(x, ffn_w_up[i], ffn_conv_w[i], ffn_conv_b[i], ffn_w_down[i])
        x = layer_norm(DEEPNORM_ALPHA * x + ffn, ln2_g[i], ln2_b[i])
        gate = jax.nn.sigmoid(x @ ple_w_gate[i] + ple_b_gate[i])
        x = x + gate * (p[i] @ ple_w_p[i])
    return x
```

```python
import functools

import jax
import jax.numpy as jnp
from jax import lax
from jax.experimental import pallas as pl
from jax.experimental.pallas import tpu as pltpu

D_MODEL = 1024
DEPTH = 4
CHUNK = 128
A_HEADS = 8
A_HEAD_DIM = D_MODEL // A_HEADS
POOL_WINDOWS = (2, 4, 8, 16)
B_GROUP_DIM = D_MODEL // len(POOL_WINDOWS)
D_FF = 2816
PLE_DIM = 256
DEEPNORM_ALPHA = (2.0 * DEPTH) ** 0.25
LN_EPS = 1e-5

ROW_TILE = 512
HALO = 8
FF_TILE = 256
VMEM_LIMIT_BYTES = 56 * 1024 * 1024
INV_SQRT2 = 0.7071067811865476


def _gelu(x):
    return 0.5 * x * (1.0 + lax.erf(x * INV_SQRT2))


def _layer_norm(x, g, b):
    mu = jnp.mean(x, axis=-1, keepdims=True)
    xc = x - mu
    var = jnp.mean(xc * xc, axis=-1, keepdims=True)
    return xc * lax.rsqrt(var + LN_EPS) * g + b


def _dot(a, b):
    return jnp.dot(a, b, preferred_element_type=jnp.float32)


def _shift_rows(x, k):
    n = x.shape[0]
    return pltpu.roll(x, (n - k) % n, axis=0)


def _assemble_rows(xf_ref, prev_ref, x_ref, next_ref):
    i = pl.program_id(1)
    last = pl.num_programs(1) - 1
    xf_ref[0:HALO, :] = jnp.where(i > 0, prev_ref[...], 0.0)
    xf_ref[HALO:HALO + ROW_TILE, :] = x_ref[...]
    xf_ref[HALO + ROW_TILE:, :] = jnp.where(i < last, next_ref[...], 0.0)


def _mixer_a_kernel(x_ref, w_in_ref, ln_g_ref, ln_b_ref, w_s_ref, bias_ref, w_o_ref,
                    ln1_g_ref, ln1_b_ref, o_ref, g_ref):
    x = x_ref[...]
    xb = x.astype(jnp.bfloat16)
    u = _gelu(_dot(xb, w_in_ref[0]))
    v = _gelu(_dot(xb, w_in_ref[1]))
    vb = _layer_norm(v, ln_g_ref[...], ln_b_ref[...]).astype(jnp.bfloat16)
    for c in range(ROW_TILE // CHUNK):
        rows = slice(c * CHUNK, (c + 1) * CHUNK)
        for h in range(A_HEADS):
            cols = slice(h * A_HEAD_DIM, (h + 1) * A_HEAD_DIM)
            s = _dot(w_s_ref[h], vb[rows, cols]) + bias_ref[:, cols]
            g_ref[rows, cols] = (u[rows, cols] * s).astype(jnp.bfloat16)
    mixed = _dot(g_ref[...], w_o_ref[...])
    o_ref[...] = _layer_norm(DEEPNORM_ALPHA * x + mixed, ln1_g_ref[...], ln1_b_ref[...])


def _const_spec(shape):
    n = len(shape)
    return pl.BlockSpec(shape, lambda b, i: (0,) * n, pipeline_mode=pl.Buffered(1))


def _row_spec(width):
    return pl.BlockSpec((None, ROW_TILE, width), lambda b, i: (b, i, 0))


def _halo_specs(seq):
    tiles = ROW_TILE // HALO
    last = seq // HALO - 1
    prev = pl.BlockSpec((None, HALO, D_MODEL),
                        lambda b, i: (b, jnp.maximum(i * tiles - 1, 0), 0))
    nxt = pl.BlockSpec((None, HALO, D_MODEL),
                       lambda b, i: (b, jnp.minimum((i + 1) * tiles, last), 0))
    return prev, nxt


def _params():
    return pltpu.CompilerParams(dimension_semantics=("parallel", "arbitrary"),
                                vmem_limit_bytes=VMEM_LIMIT_BYTES)


def _mixer_a(x, w_in, ln_g, ln_b, w_s, bias, w_o, ln1_g, ln1_b):
    bsz, seq, d = x.shape
    return pl.pallas_call(
        _mixer_a_kernel,
        grid=(bsz, seq // ROW_TILE),
        in_specs=[_row_spec(d), _const_spec(w_in.shape), _const_spec(ln_g.shape),
                  _const_spec(ln_b.shape), _const_spec(w_s.shape), _const_spec(bias.shape),
                  _const_spec(w_o.shape), _const_spec(ln1_g.shape), _const_spec(ln1_b.shape)],
        out_specs=_row_spec(d),
        out_shape=jax.ShapeDtypeStruct(x.shape, jnp.float32),
        scratch_shapes=[pltpu.VMEM((ROW_TILE, d), jnp.bfloat16)],
        compiler_params=_params(),
        name="mixer_a",
    )(x, w_in, ln_g, ln_b, w_s, bias, w_o, ln1_g, ln1_b)


def _mixer_b_kernel(seq, prev_ref, x_ref, next_ref, w_in_ref, w_grp_ref, scale_ref, w_o_ref,
                    ln1_g_ref, ln1_b_ref, o_ref, xf_ref, y_ref):
    _assemble_rows(xf_ref, prev_ref, x_ref, next_ref)
    h = _dot(xf_ref[...].astype(jnp.bfloat16), w_in_ref[...])
    t = pl.program_id(1) * ROW_TILE + lax.broadcasted_iota(jnp.int32, (ROW_TILE, 1), 0)
    for g, window in enumerate(POOL_WINDOWS):
        cols = slice(g * B_GROUP_DIM, (g + 1) * B_GROUP_DIM)
        hg = h[:, cols]
        run, span = hg, 1
        while span < window:
            run = run + _shift_rows(run, span)
            span *= 2
        half = window // 2
        total = run[HALO - half:HALO - half + ROW_TILE, :]
        lo = jnp.maximum(t - half, 0)
        hi = jnp.minimum(t + (window - half - 1), seq - 1)
        count = (hi - lo + 1).astype(jnp.float32)
        pooled = total / count - hg[HALO:HALO + ROW_TILE, :]
        y = _dot(pooled.astype(jnp.bfloat16), w_grp_ref[g])
        y_ref[:, cols] = (y * scale_ref[:, cols]).astype(jnp.bfloat16)
    mixed = _dot(y_ref[...], w_o_ref[...])
    o_ref[...] = _layer_norm(DEEPNORM_ALPHA * x_ref[...] + mixed, ln1_g_ref[...], ln1_b_ref[...])


def _mixer_b(x, w_in, w_grp, scale, w_o, ln1_g, ln1_b):
    bsz, seq, d = x.shape
    prev, nxt = _halo_specs(seq)
    return pl.pallas_call(
        functools.partial(_mixer_b_kernel, seq),
        grid=(bsz, seq // ROW_TILE),
        in_specs=[prev, _row_spec(d), nxt, _const_spec(w_in.shape), _const_spec(w_grp.shape),
                  _const_spec(scale.shape), _const_spec(w_o.shape), _const_spec(ln1_g.shape),
                  _const_spec(ln1_b.shape)],
        out_specs=_row_spec(d),
        out_shape=jax.ShapeDtypeStruct(x.shape, jnp.float32),
        scratch_shapes=[pltpu.VMEM((ROW_TILE + 2 * HALO, d), jnp.float32),
                        pltpu.VMEM((ROW_TILE, d), jnp.bfloat16)],
        compiler_params=_params(),
        name="mixer_b",
    )(x, x, x, w_in, w_grp, scale, w_o, ln1_g, ln1_b)


def _ffn_kernel(prev_ref, x_ref, next_ref, p_ref, w_up_ref, conv_w_ref, conv_b_ref, w_down_ref,
                ln2_g_ref, ln2_b_ref, w_gate_ref, b_gate_ref, w_p_ref, o_ref, xf_ref, a_ref):
    _assemble_rows(xf_ref, prev_ref, x_ref, next_ref)
    xb = xf_ref[...].astype(jnp.bfloat16)
    n_tiles = D_FF // FF_TILE

    def conv(h, j):
        cw = conv_w_ref[j]
        c = (cw[0:1, :] * _shift_rows(h, -1) + cw[1:2, :] * h + cw[2:3, :] * _shift_rows(h, 1))
        return c[HALO:HALO + ROW_TILE, :] + conv_b_ref[j]

    for j in range(n_tiles):
        gate = conv(_dot(xb, w_up_ref[j]), j)
        val = conv(_dot(xb, w_up_ref[n_tiles + j]), n_tiles + j)
        a_ref[:, j * FF_TILE:(j + 1) * FF_TILE] = (_gelu(gate) * val).astype(jnp.bfloat16)

    ffn = _dot(a_ref[...], w_down_ref[...])
    x2 = _layer_norm(DEEPNORM_ALPHA * x_ref[...] + ffn, ln2_g_ref[...], ln2_b_ref[...])
    gate = jax.nn.sigmoid(_dot(x2.astype(jnp.bfloat16), w_gate_ref[...]) + b_gate_ref[...])
    emb = _dot(p_ref[...].astype(jnp.bfloat16), w_p_ref[...])
    o_ref[...] = x2 + gate * emb


def _ffn(x, p, w_up, conv_w, conv_b, w_down, ln2_g, ln2_b, w_gate, b_gate, w_p):
    bsz, seq, d = x.shape
    prev, nxt = _halo_specs(seq)
    return pl.pallas_call(
        _ffn_kernel,
        grid=(bsz, seq // ROW_TILE),
        in_specs=[prev, _row_spec(d), nxt, _row_spec(PLE_DIM), _const_spec(w_up.shape),
                  _const_spec(conv_w.shape), _const_spec(conv_b.shape), _const_spec(w_down.shape),
                  _const_spec(ln2_g.shape), _const_spec(ln2_b.shape), _const_spec(w_gate.shape),
                  _const_spec(b_gate.shape), _const_spec(w_p.shape)],
        out_specs=_row_spec(d),
        out_shape=jax.ShapeDtypeStruct(x.shape, jnp.float32),
        scratch_shapes=[pltpu.VMEM((ROW_TILE + 2 * HALO, d), jnp.float32),
                        pltpu.VMEM((ROW_TILE, D_FF), jnp.bfloat16)],
        compiler_params=_params(),
        name="conv_ffn",
    )(x, x, x, p, w_up, conv_w, conv_b, w_down, ln2_g, ln2_b, w_gate, b_gate, w_p)


def _bf16(w):
    return w.astype(jnp.bfloat16)


def _ff_tiles(w):
    lead = w.shape[:-1]
    w = w.reshape(lead + (2 * D_FF // FF_TILE, FF_TILE))
    return jnp.moveaxis(w, -2, 0)


def kernel(x, p, a_w_in, a_ln_g, a_ln_b, a_w_s, a_b_s, a_w_o, b_w_in, b_w_grp, b_scale, b_w_o,
           ffn_w_up, ffn_conv_w, ffn_conv_b, ffn_w_down, ln1_g, ln1_b, ln2_g, ln2_b,
           ple_w_p, ple_w_gate, ple_b_gate):
    row = lambda v: v.reshape(1, -1)
    for i in range(DEPTH):
        j = i // 2
        if i % 2 == 0:
            w_in = _bf16(jnp.stack(jnp.split(a_w_in[j], 2, axis=-1)))
            bias = jnp.repeat(a_b_s[j].T, A_HEAD_DIM, axis=1)
            x = _mixer_a(x, w_in, row(a_ln_g[j]), row(a_ln_b[j]), _bf16(a_w_s[j]), bias,
                         _bf16(a_w_o[j]), row(ln1_g[i]), row(ln1_b[i]))
        else:
            x = _mixer_b(x, _bf16(b_w_in[j]), _bf16(b_w_grp[j]), row(b_scale[j]),
                         _bf16(b_w_o[j]), row(ln1_g[i]), row(ln1_b[i]))
        x = _ffn(x, p[i], _ff_tiles(_bf16(ffn_w_up[i])), _ff_tiles(ffn_conv_w[i]),
                 _ff_tiles(row(ffn_conv_b[i])), _bf16(ffn_w_down[i]), row(ln2_g[i]),
                 row(ln2_b[i]), _bf16(ple_w_gate[i]), row(ple_b_gate[i]), _bf16(ple_w_p[i]))
    return x
```

```python
import functools

import jax
import jax.numpy as jnp
from jax import lax
from jax.experimental import pallas as pl
from jax.experimental.pallas import tpu as pltpu

D_MODEL = 1024
DEPTH = 4
CHUNK = 128
A_HEADS = 8
A_HEAD_DIM = D_MODEL // A_HEADS
POOL_WINDOWS = (2, 4, 8, 16)
B_GROUP_DIM = D_MODEL // len(POOL_WINDOWS)
D_FF = 2816
PLE_DIM = 256
DEEPNORM_ALPHA = (2.0 * DEPTH) ** 0.25
LN_EPS = 1e-5

MIX_ROW_TILE = 1024
MIX_SUB_ROWS = 512
FFN_ROW_TILE = 512
HALO = 8
FF_TILE = 256
VMEM_LIMIT_BYTES = 56 * 1024 * 1024
INV_SQRT2 = 0.7071067811865476


def _gelu(x):
    return 0.5 * x * (1.0 + lax.erf(x * INV_SQRT2))


def _layer_norm(x, g, b):
    mu = jnp.mean(x, axis=-1, keepdims=True)
    xc = x - mu
    var = jnp.mean(xc * xc, axis=-1, keepdims=True)
    return xc * lax.rsqrt(var + LN_EPS) * g + b


def _dot(a, b):
    return jnp.dot(a, b, preferred_element_type=jnp.float32)


def _shift_rows(x, k):
    n = x.shape[0]
    return pltpu.roll(x, (n - k) % n, axis=0)


def _halo_rows(prev_ref, next_ref):
    i = pl.program_id(1)
    last = pl.num_programs(1) - 1
    return jnp.where(i > 0, prev_ref[...], 0.0), jnp.where(i < last, next_ref[...], 0.0)


def _layer_spec(shape, slot):
    n = len(shape) - 1
    return pl.BlockSpec((None,) + tuple(shape[1:]), lambda b, i, idx: (idx[slot],) + (0,) * n,
                        pipeline_mode=pl.Buffered(1))


def _row_spec(rows, width):
    return pl.BlockSpec((None, rows, width), lambda b, i, idx: (b, i, 0))


def _halo_specs(rows, seq):
    tiles = rows // HALO
    last = seq // HALO - 1
    prev = pl.BlockSpec((None, HALO, D_MODEL),
                        lambda b, i, idx: (b, jnp.maximum(i * tiles - 1, 0), 0))
    nxt = pl.BlockSpec((None, HALO, D_MODEL),
                       lambda b, i, idx: (b, jnp.minimum((i + 1) * tiles, last), 0))
    return prev, nxt


def _call(body, name, x, rows, in_specs, scratch_shapes, idx, operands):
    bsz, seq, d = x.shape
    grid_spec = pltpu.PrefetchScalarGridSpec(
        num_scalar_prefetch=1, grid=(bsz, seq // rows), in_specs=in_specs,
        out_specs=_row_spec(rows, d), scratch_shapes=scratch_shapes)
    return pl.pallas_call(
        body, grid_spec=grid_spec, out_shape=jax.ShapeDtypeStruct(x.shape, jnp.float32),
        compiler_params=pltpu.CompilerParams(dimension_semantics=("parallel", "arbitrary"),
                                             vmem_limit_bytes=VMEM_LIMIT_BYTES),
        name=name,
    )(idx, *operands)


def _mixer_a_kernel(idx_ref, x_ref, w_in_ref, ln_g_ref, ln_b_ref, w_s_ref, bias_ref, w_o_ref,
                    ln1_g_ref, ln1_b_ref, o_ref, g_ref):
    for r0 in range(0, MIX_ROW_TILE, MIX_SUB_ROWS):
        x = x_ref[r0:r0 + MIX_SUB_ROWS, :]
        xb = x.astype(jnp.bfloat16)
        u = _gelu(_dot(xb, w_in_ref[:, :D_MODEL]))
        v = _gelu(_dot(xb, w_in_ref[:, D_MODEL:]))
        vb = _layer_norm(v, ln_g_ref[...], ln_b_ref[...]).astype(jnp.bfloat16)
        for c in range(MIX_SUB_ROWS // CHUNK):
            rows = slice(c * CHUNK, (c + 1) * CHUNK)
            for h in range(A_HEADS):
                cols = slice(h * A_HEAD_DIM, (h + 1) * A_HEAD_DIM)
                s = _dot(w_s_ref[h], vb[rows, cols]) + bias_ref[:, cols]
                g_ref[r0 + c * CHUNK:r0 + (c + 1) * CHUNK, cols] = (
                    u[rows, cols] * s).astype(jnp.bfloat16)
        mixed = _dot(g_ref[r0:r0 + MIX_SUB_ROWS, :], w_o_ref[...])
        o_ref[r0:r0 + MIX_SUB_ROWS, :] = _layer_norm(
            DEEPNORM_ALPHA * x + mixed, ln1_g_ref[...], ln1_b_ref[...])


def _mixer_a(x, idx, w_in, ln_g, ln_b, w_s, bias, w_o, ln1_g, ln1_b):
    d = x.shape[-1]
    in_specs = [_row_spec(MIX_ROW_TILE, d), _layer_spec(w_in.shape, 0),
                _layer_spec(ln_g.shape, 0), _layer_spec(ln_b.shape, 0),
                _layer_spec(w_s.shape, 0), _layer_spec(bias.shape, 0), _layer_spec(w_o.shape, 0),
                _layer_spec(ln1_g.shape, 1), _layer_spec(ln1_b.shape, 1)]
    scratch = [pltpu.VMEM((MIX_ROW_TILE, d), jnp.bfloat16)]
    return _call(_mixer_a_kernel, "mixer_a", x, MIX_ROW_TILE, in_specs, scratch, idx,
                 (x, w_in, ln_g, ln_b, w_s, bias, w_o, ln1_g, ln1_b))


def _mixer_b_kernel(seq, idx_ref, prev_ref, x_ref, next_ref, w_in_ref, w_grp_ref, scale_ref,
                    w_o_ref, ln1_g_ref, ln1_b_ref, o_ref, xf_ref, y_ref):
    prev, nxt = _halo_rows(prev_ref, next_ref)
    xf_ref[0:HALO, :] = prev
    xf_ref[HALO:HALO + MIX_ROW_TILE, :] = x_ref[...]
    xf_ref[HALO + MIX_ROW_TILE:, :] = nxt
    sub = MIX_SUB_ROWS
    for r0 in range(0, MIX_ROW_TILE, sub):
        h = _dot(xf_ref[r0:r0 + sub + 2 * HALO, :].astype(jnp.bfloat16), w_in_ref[...])
        t = (pl.program_id(1) * MIX_ROW_TILE + r0
             + lax.broadcasted_iota(jnp.int32, (sub, 1), 0))
        for g, window in enumerate(POOL_WINDOWS):
            cols = slice(g * B_GROUP_DIM, (g + 1) * B_GROUP_DIM)
            hg = h[:, cols]
            run, span = hg, 1
            while span < window:
                run = run + _shift_rows(run, span)
                span *= 2
            half = window // 2
            total = run[HALO - half:HALO - half + sub, :]
            lo = jnp.maximum(t - half, 0)
            hi = jnp.minimum(t + (window - half - 1), seq - 1)
            count = (hi - lo + 1).astype(jnp.float32)
            pooled = total / count - hg[HALO:HALO + sub, :]
            y = _dot(pooled.astype(jnp.bfloat16), w_grp_ref[g])
            y_ref[r0:r0 + sub, cols] = (y * scale_ref[:, cols]).astype(jnp.bfloat16)
        mixed = _dot(y_ref[r0:r0 + sub, :], w_o_ref[...])
        o_ref[r0:r0 + sub, :] = _layer_norm(
            DEEPNORM_ALPHA * x_ref[r0:r0 + sub, :] + mixed, ln1_g_ref[...], ln1_b_ref[...])


def _mixer_b(x, idx, w_in, w_grp, scale, w_o, ln1_g, ln1_b):
    seq, d = x.shape[1:]
    prev, nxt = _halo_specs(MIX_ROW_TILE, seq)
    in_specs = [prev, _row_spec(MIX_ROW_TILE, d), nxt, _layer_spec(w_in.shape, 0),
                _layer_spec(w_grp.shape, 0), _layer_spec(scale.shape, 0),
                _layer_spec(w_o.shape, 0), _layer_spec(ln1_g.shape, 1),
                _layer_spec(ln1_b.shape, 1)]
    scratch = [pltpu.VMEM((MIX_ROW_TILE + 2 * HALO, d), jnp.float32),
               pltpu.VMEM((MIX_ROW_TILE, d), jnp.bfloat16)]
    return _call(functools.partial(_mixer_b_kernel, seq), "mixer_b", x, MIX_ROW_TILE, in_specs,
                 scratch, idx, (x, x, x, w_in, w_grp, scale, w_o, ln1_g, ln1_b))


def _ffn_kernel(idx_ref, prev_ref, x_ref, next_ref, p_ref, w_up_ref, conv_w_ref, conv_b_ref,
                w_down_ref, ln2_g_ref, ln2_b_ref, w_gate_ref, b_gate_ref, w_p_ref, o_ref,
                xb_ref, a_ref):
    rows = FFN_ROW_TILE
    prev, nxt = _halo_rows(prev_ref, next_ref)
    xb_ref[0:rows, :] = x_ref[...].astype(jnp.bfloat16)
    xb_ref[rows:, :] = jnp.concatenate([nxt, prev], axis=0).astype(jnp.bfloat16)

    def up_conv(col0):
        cols = slice(col0, col0 + FF_TILE)
        h = _dot(xb_ref[...], w_up_ref[:, cols])
        c = (conv_w_ref[0:1, cols] * _shift_rows(h, -1) + conv_w_ref[1:2, cols] * h
             + conv_w_ref[2:3, cols] * _shift_rows(h, 1))
        return c[0:rows, :] + conv_b_ref[:, cols]

    for j in range(D_FF // FF_TILE):
        gate = up_conv(j * FF_TILE)
        val = up_conv(D_FF + j * FF_TILE)
        a_ref[:, j * FF_TILE:(j + 1) * FF_TILE] = (_gelu(gate) * val).astype(jnp.bfloat16)

    ffn = _dot(a_ref[...], w_down_ref[...])
    x2 = _layer_norm(DEEPNORM_ALPHA * x_ref[...] + ffn, ln2_g_ref[...], ln2_b_ref[...])
    gate = jax.nn.sigmoid(_dot(x2.astype(jnp.bfloat16), w_gate_ref[...]) + b_gate_ref[...])
    emb = _dot(p_ref[...].astype(jnp.bfloat16), w_p_ref[...])
    o_ref[...] = x2 + gate * emb


def _ffn(x, idx, p, w_up, conv_w, conv_b, w_down, ln2_g, ln2_b, w_gate, b_gate, w_p):
    seq, d = x.shape[1:]
    rows = FFN_ROW_TILE
    prev, nxt = _halo_specs(rows, seq)
    p_spec = pl.BlockSpec((None, None, rows, PLE_DIM), lambda b, i, idx: (idx[1], b, i, 0))
    in_specs = [prev, _row_spec(rows, d), nxt, p_spec, _layer_spec(w_up.shape, 1),
                _layer_spec(conv_w.shape, 1), _layer_spec(conv_b.shape, 1),
                _layer_spec(w_down.shape, 1), _layer_spec(ln2_g.shape, 1),
                _layer_spec(ln2_b.shape, 1), _layer_spec(w_gate.shape, 1),
                _layer_spec(b_gate.shape, 1), _layer_spec(w_p.shape, 1)]
    scratch = [pltpu.VMEM((rows + 2 * HALO, d), jnp.bfloat16),
               pltpu.VMEM((rows, D_FF), jnp.bfloat16)]
    return _call(_ffn_kernel, "conv_ffn", x, rows, in_specs, scratch, idx,
                 (x, x, x, p, w_up, conv_w, conv_b, w_down, ln2_g, ln2_b, w_gate, b_gate, w_p))


def kernel(x, p, a_w_in, a_ln_g, a_ln_b, a_w_s, a_b_s, a_w_o, b_w_in, b_w_grp, b_scale, b_w_o,
           ffn_w_up, ffn_conv_w, ffn_conv_b, ffn_w_down, ln1_g, ln1_b, ln2_g, ln2_b,
           ple_w_p, ple_w_gate, ple_b_gate):
    bf16 = lambda w: w.astype(jnp.bfloat16)
    rows = lambda v: v[:, None, :]
    a_w_in, a_w_s, a_w_o = bf16(a_w_in), bf16(a_w_s), bf16(a_w_o)
    b_w_in, b_w_grp, b_w_o = bf16(b_w_in), bf16(b_w_grp), bf16(b_w_o)
    ffn_w_up, ffn_w_down = bf16(ffn_w_up), bf16(ffn_w_down)
    ple_w_gate, ple_w_p = bf16(ple_w_gate), bf16(ple_w_p)
    a_ln_g, a_ln_b, b_scale = rows(a_ln_g), rows(a_ln_b), rows(b_scale)
    ln1_g, ln1_b, ln2_g, ln2_b = rows(ln1_g), rows(ln1_b), rows(ln2_g), rows(ln2_b)
    ffn_conv_b, ple_b_gate = rows(ffn_conv_b), rows(ple_b_gate)
    a_bias = jnp.repeat(jnp.swapaxes(a_b_s, 1, 2), A_HEAD_DIM, axis=2)
    for i in range(DEPTH):
        idx = jnp.array([i // 2, i], jnp.int32)
        if i % 2 == 0:
            x = _mixer_a(x, idx, a_w_in, a_ln_g, a_ln_b, a_w_s, a_bias, a_w_o, ln1_g, ln1_b)
        else:
            x = _mixer_b(x, idx, b_w_in, b_w_grp, b_scale, b_w_o, ln1_g, ln1_b)
        x = _ffn(x, idx, p, ffn_w_up, ffn_conv_w, ffn_conv_b, ffn_w_down, ln2_g, ln2_b,
                 ple_w_gate, ple_b_gate, ple_w_p)
    return x
```

```python
import functools

import jax
import jax.numpy as jnp
from jax import lax
from jax.experimental import pallas as pl
from jax.experimental.pallas import tpu as pltpu

D_MODEL = 1024
DEPTH = 4
CHUNK = 128
A_HEADS = 8
A_HEAD_DIM = D_MODEL // A_HEADS
POOL_WINDOWS = (2, 4, 8, 16)
B_GROUP_DIM = D_MODEL // len(POOL_WINDOWS)
D_FF = 2816
PLE_DIM = 256
DEEPNORM_ALPHA = (2.0 * DEPTH) ** 0.25
LN_EPS = 1e-5

MIX_ROW_TILE = 1024
MIX_SUB_ROWS = 256
FFN_ROW_TILE = 512
SUBLANES = 8
LANES = 128
HALO = SUBLANES
FF_TILE = 256
VMEM_LIMIT_BYTES = 56 * 1024 * 1024
INV_SQRT2 = 0.7071067811865476


def _gelu_x2(x):
    return x * (1.0 + lax.erf(x * INV_SQRT2))


def _layer_norm(x, g, b, eps=LN_EPS):
    mu = jnp.mean(x, axis=-1, keepdims=True)
    xc = x - mu
    var = jnp.mean(xc * xc, axis=-1, keepdims=True)
    return xc * lax.rsqrt(var + eps) * g + b


def _software_pipeline(n, stages):
    results = [[None] * n for _ in stages]
    for step in range(n + len(stages) - 1):
        for j, stage in enumerate(stages):
            k = step - j
            if 0 <= k < n:
                results[j][k] = stage(k, results[j - 1][k] if j else None)


def _dot(a, b):
    return jnp.dot(a, b, preferred_element_type=jnp.float32)


def _shift_rows(x, k):
    n = x.shape[0]
    return pltpu.roll(x, (n - k) % n, axis=0)


def _halo_rows(prev_ref, next_ref):
    i = pl.program_id(1)
    last = pl.num_programs(1) - 1
    return jnp.where(i > 0, prev_ref[...], 0.0), jnp.where(i < last, next_ref[...], 0.0)


def _layer_spec(shape, slot):
    n = len(shape) - 1
    return pl.BlockSpec((None,) + tuple(shape[1:]), lambda b, i, idx: (idx[slot],) + (0,) * n,
                        pipeline_mode=pl.Buffered(1))


def _row_spec(rows, width):
    return pl.BlockSpec((None, rows, width), lambda b, i, idx: (b, i, 0))


def _halo_specs(rows, seq):
    tiles = rows // HALO
    last = seq // HALO - 1
    prev = pl.BlockSpec((None, HALO, D_MODEL),
                        lambda b, i, idx: (b, jnp.maximum(i * tiles - 1, 0), 0))
    nxt = pl.BlockSpec((None, HALO, D_MODEL),
                       lambda b, i, idx: (b, jnp.minimum((i + 1) * tiles, last), 0))
    return prev, nxt


def _call(body, name, x, rows, in_specs, scratch_shapes, idx, operands):
    bsz, seq, d = x.shape
    grid_spec = pltpu.PrefetchScalarGridSpec(
        num_scalar_prefetch=1, grid=(bsz, seq // rows), in_specs=in_specs,
        out_specs=_row_spec(rows, d), scratch_shapes=scratch_shapes)
    return pl.pallas_call(
        body, grid_spec=grid_spec, out_shape=jax.ShapeDtypeStruct(x.shape, jnp.float32),
        compiler_params=pltpu.CompilerParams(dimension_semantics=("parallel", "arbitrary"),
                                             vmem_limit_bytes=VMEM_LIMIT_BYTES),
        name=name,
    )(idx, *operands)


def _sub_rows(k):
    return slice(k * MIX_SUB_ROWS, (k + 1) * MIX_SUB_ROWS)


def _mixer_a_kernel(idx_ref, x_ref, w_in_ref, ln_g_ref, ln_b_ref, w_s_ref, bias_ref, w_o_ref,
                    ln1_g_ref, ln1_b_ref, o_ref, u_ref, vb_ref, g_ref):
    def project(k, _):
        rows = _sub_rows(k)
        xb = x_ref[rows, :].astype(jnp.bfloat16)
        u_ref[rows, :] = _gelu_x2(_dot(xb, w_in_ref[:, :D_MODEL]))
        v = _gelu_x2(_dot(xb, w_in_ref[:, D_MODEL:]))
        vb_ref[rows, :] = _layer_norm(v, ln_g_ref[...], ln_b_ref[...],
                                      eps=4.0 * LN_EPS).astype(jnp.bfloat16)

    def mix(k, _):
        rows = _sub_rows(k)
        for r0 in range(rows.start, rows.stop, CHUNK):
            chunk = slice(r0, r0 + CHUNK)
            for h in range(A_HEADS):
                cols = slice(h * A_HEAD_DIM, (h + 1) * A_HEAD_DIM)
                s = _dot(w_s_ref[h], vb_ref[chunk, cols]) + bias_ref[:, cols]
                g_ref[chunk, cols] = (u_ref[chunk, cols] * s).astype(jnp.bfloat16)
        return _dot(g_ref[rows, :], w_o_ref[...])

    def finish(k, mixed):
        rows = _sub_rows(k)
        o_ref[rows, :] = _layer_norm(DEEPNORM_ALPHA * x_ref[rows, :] + mixed,
                                     ln1_g_ref[...], ln1_b_ref[...])

    _software_pipeline(MIX_ROW_TILE // MIX_SUB_ROWS, [project, mix, finish])


def _mixer_a(x, idx, w_in, ln_g, ln_b, w_s, bias, w_o, ln1_g, ln1_b):
    d = x.shape[-1]
    in_specs = [_row_spec(MIX_ROW_TILE, d), _layer_spec(w_in.shape, 0),
                _layer_spec(ln_g.shape, 0), _layer_spec(ln_b.shape, 0),
                _layer_spec(w_s.shape, 0), _layer_spec(bias.shape, 0), _layer_spec(w_o.shape, 0),
                _layer_spec(ln1_g.shape, 1), _layer_spec(ln1_b.shape, 1)]
    scratch = [pltpu.VMEM((MIX_ROW_TILE, d), jnp.float32),
               pltpu.VMEM((MIX_ROW_TILE, d), jnp.bfloat16),
               pltpu.VMEM((MIX_ROW_TILE, d), jnp.bfloat16)]
    return _call(_mixer_a_kernel, "mixer_a", x, MIX_ROW_TILE, in_specs, scratch, idx,
                 (x, w_in, ln_g, ln_b, w_s, bias, w_o, ln1_g, ln1_b))


def _mixer_b_kernel(seq, idx_ref, prev_ref, x_ref, next_ref, w_in_ref, w_grp_ref, scale_ref,
                    w_o_ref, ln1_g_ref, ln1_b_ref, o_ref, xf_ref, pooled_ref, y_ref):
    prev, nxt = _halo_rows(prev_ref, next_ref)
    xf_ref[0:HALO, :] = prev
    xf_ref[HALO:HALO + MIX_ROW_TILE, :] = x_ref[...]
    xf_ref[HALO + MIX_ROW_TILE:, :] = nxt
    sub = MIX_SUB_ROWS

    def project(k, _):
        rows = _sub_rows(k)
        h = _dot(xf_ref[rows.start:rows.stop + 2 * HALO, :].astype(jnp.bfloat16), w_in_ref[...])
        t = (pl.program_id(1) * MIX_ROW_TILE + rows.start
             + lax.broadcasted_iota(jnp.int32, (sub, 1), 0))
        for g, window in enumerate(POOL_WINDOWS):
            cols = slice(g * B_GROUP_DIM, (g + 1) * B_GROUP_DIM)
            hg = h[:, cols]
            run, span = hg, 1
            while span < window:
                run = run + _shift_rows(run, span)
                span *= 2
            half = window // 2
            total = run[HALO - half:HALO - half + sub, :]
            lo = jnp.maximum(t - half, 0)
            hi = jnp.minimum(t + (window - half - 1), seq - 1)
            count = (hi - lo + 1).astype(jnp.float32)
            pooled = total / count - hg[HALO:HALO + sub, :]
            pooled_ref[rows, cols] = pooled.astype(jnp.bfloat16)

    def mix(k, _):
        rows = _sub_rows(k)
        for g in range(len(POOL_WINDOWS)):
            cols = slice(g * B_GROUP_DIM, (g + 1) * B_GROUP_DIM)
            y = _dot(pooled_ref[rows, cols], w_grp_ref[g])
            y_ref[rows, cols] = (y * scale_ref[:, cols]).astype(jnp.bfloat16)
        return _dot(y_ref[rows, :], w_o_ref[...])

    def finish(k, mixed):
        rows = _sub_rows(k)
        o_ref[rows, :] = _layer_norm(DEEPNORM_ALPHA * x_ref[rows, :] + mixed,
                                     ln1_g_ref[...], ln1_b_ref[...])

    _software_pipeline(MIX_ROW_TILE // MIX_SUB_ROWS, [project, mix, finish])


def _mixer_b(x, idx, w_in, w_grp, scale, w_o, ln1_g, ln1_b):
    seq, d = x.shape[1:]
    prev, nxt = _halo_specs(MIX_ROW_TILE, seq)
    in_specs = [prev, _row_spec(MIX_ROW_TILE, d), nxt, _layer_spec(w_in.shape, 0),
                _layer_spec(w_grp.shape, 0), _layer_spec(scale.shape, 0),
                _layer_spec(w_o.shape, 0), _layer_spec(ln1_g.shape, 1),
                _layer_spec(ln1_b.shape, 1)]
    scratch = [pltpu.VMEM((MIX_ROW_TILE + 2 * HALO, d), jnp.float32),
               pltpu.VMEM((MIX_ROW_TILE, d), jnp.bfloat16),
               pltpu.VMEM((MIX_ROW_TILE, d), jnp.bfloat16)]
    return _call(functools.partial(_mixer_b_kernel, seq), "mixer_b", x, MIX_ROW_TILE, in_specs,
                 scratch, idx, (x, x, x, w_in, w_grp, scale, w_o, ln1_g, ln1_b))


def _ffn_kernel(idx_ref, prev_ref, x_ref, next_ref, p_ref, w_up_ref, conv_w_ref, conv_b_ref,
                w_down_ref, ln2_g_ref, ln2_b_ref, w_gate_ref, b_gate_ref, w_p_ref, o_ref,
                xs_ref, ps_ref, os_ref, xb_ref, a_ref):
    rows = FFN_ROW_TILE
    ext = rows + 2 * HALO
    groups = ext // SUBLANES
    prev, nxt = _halo_rows(prev_ref, next_ref)
    for c in range(D_MODEL // LANES):
        lanes = slice(c * LANES, (c + 1) * LANES)
        xs_ref[c, 0:HALO, :] = prev[:, lanes]
        xs_ref[c, HALO:HALO + rows, :] = x_ref[:, lanes]
        xs_ref[c, HALO + rows:, :] = nxt[:, lanes]
    zeros = jnp.zeros((HALO, LANES), jnp.float32)
    for c in range(PLE_DIM // LANES):
        lanes = slice(c * LANES, (c + 1) * LANES)
        ps_ref[c, 0:HALO, :] = zeros
        ps_ref[c, HALO:HALO + rows, :] = p_ref[:, lanes]
        ps_ref[c, HALO + rows:, :] = zeros

    def seq_rows(ref, i0, n, width):
        return jnp.concatenate(
            [jnp.concatenate([ref[c, pl.ds(i, SUBLANES, stride=groups), :]
                              for c in range(width // LANES)], axis=1)
             for i in range(i0, i0 + n)], axis=0)

    for i in range(0, groups, 2):
        xb_ref[i * SUBLANES:(i + 2) * SUBLANES, :] = seq_rows(xs_ref, i, 2, D_MODEL).astype(
            jnp.bfloat16)

    def up_conv(col0):
        cols = slice(col0, col0 + FF_TILE)
        h = _dot(xb_ref[...], w_up_ref[:, cols])
        last = ext - SUBLANES
        below = jnp.concatenate([pltpu.roll(h[last:, :], 1, axis=0), h[:last, :]], axis=0)
        above = jnp.concatenate([h[SUBLANES:, :], pltpu.roll(h[:SUBLANES, :], SUBLANES - 1, axis=0)],
                                axis=0)
        return (conv_w_ref[0:1, cols] * below + conv_w_ref[1:2, cols] * h
                + conv_w_ref[2:3, cols] * above + conv_b_ref[:, cols])

    for j in range(D_FF // FF_TILE):
        gate = up_conv(j * FF_TILE)
        val = up_conv(D_FF + j * FF_TILE)
        a_ref[:, j * FF_TILE:(j + 1) * FF_TILE] = (_gelu_x2(gate) * val).astype(jnp.bfloat16)

    split = (ext // 2 + 15) // 16 * 16
    blocks = [slice(0, split), slice(split, ext)]

    def down_project(k, _):
        blk = blocks[k]
        g0, n = blk.start // SUBLANES, (blk.stop - blk.start) // SUBLANES
        y = (DEEPNORM_ALPHA * seq_rows(xs_ref, g0, n, D_MODEL)
             + _dot(a_ref[blk, :], w_down_ref[...]))
        emb = _dot(seq_rows(ps_ref, g0, n, PLE_DIM).astype(jnp.bfloat16), w_p_ref[...])
        return y, emb

    def norm_gate(k, y_emb):
        y, emb = y_emb
        x2 = _layer_norm(y, ln2_g_ref[...], ln2_b_ref[...])
        pre = _dot(x2.astype(jnp.bfloat16), w_gate_ref[...]) + b_gate_ref[...]
        return x2, pre, emb

    def embed(k, parts):
        x2, pre, emb = parts
        out = x2 + jax.nn.sigmoid(pre) * emb
        g0 = blocks[k].start // SUBLANES
        for i in range(out.shape[0] // SUBLANES):
            for c in range(D_MODEL // LANES):
                os_ref[c, pl.ds(g0 + i, SUBLANES, stride=groups), :] = out[
                    i * SUBLANES:(i + 1) * SUBLANES, c * LANES:(c + 1) * LANES]

    _software_pipeline(len(blocks), [down_project, norm_gate, embed])
    for c in range(D_MODEL // LANES):
        o_ref[:, c * LANES:(c + 1) * LANES] = os_ref[c, HALO:HALO + rows, :]


def _ffn(x, idx, p, w_up, conv_w, conv_b, w_down, ln2_g, ln2_b, w_gate, b_gate, w_p):
    seq, d = x.shape[1:]
    rows = FFN_ROW_TILE
    ext = rows + 2 * HALO
    prev, nxt = _halo_specs(rows, seq)
    p_spec = pl.BlockSpec((None, None, rows, PLE_DIM), lambda b, i, idx: (idx[1], b, i, 0))
    in_specs = [prev, _row_spec(rows, d), nxt, p_spec, _layer_spec(w_up.shape, 1),
                _layer_spec(conv_w.shape, 1), _layer_spec(conv_b.shape, 1),
                _layer_spec(w_down.shape, 1), _layer_spec(ln2_g.shape, 1),
                _layer_spec(ln2_b.shape, 1), _layer_spec(w_gate.shape, 1),
                _layer_spec(b_gate.shape, 1), _layer_spec(w_p.shape, 1)]
    scratch = [pltpu.VMEM((d // LANES, ext, LANES), jnp.float32),
               pltpu.VMEM((PLE_DIM // LANES, ext, LANES), jnp.float32),
               pltpu.VMEM((d // LANES, ext, LANES), jnp.float32),
               pltpu.VMEM((ext, d), jnp.bfloat16),
               pltpu.VMEM((ext, D_FF), jnp.bfloat16)]
    return _call(_ffn_kernel, "conv_ffn", x, rows, in_specs, scratch, idx,
                 (x, x, x, p, w_up, conv_w, conv_b, w_down, ln2_g, ln2_b, w_gate, b_gate, w_p))


def kernel(x, p, a_w_in, a_ln_g, a_ln_b, a_w_s, a_b_s, a_w_o, b_w_in, b_w_grp, b_scale, b_w_o,
           ffn_w_up, ffn_conv_w, ffn_conv_b, ffn_w_down, ln1_g, ln1_b, ln2_g, ln2_b,
           ple_w_p, ple_w_gate, ple_b_gate):
    bf16 = lambda w: w.astype(jnp.bfloat16)
    rows = lambda v: v[:, None, :]
    a_w_in, a_w_s, a_w_o = bf16(a_w_in), bf16(0.5 * a_w_s), bf16(a_w_o)
    b_w_in, b_w_grp, b_w_o = bf16(b_w_in), bf16(b_w_grp), bf16(b_w_o)
    ffn_w_up, ffn_w_down = bf16(ffn_w_up), bf16(0.5 * ffn_w_down)
    ple_w_gate, ple_w_p = bf16(ple_w_gate), bf16(ple_w_p)
    a_ln_g, a_ln_b, b_scale = rows(a_ln_g), rows(a_ln_b), rows(b_scale)
    ln1_g, ln1_b, ln2_g, ln2_b = rows(ln1_g), rows(ln1_b), rows(ln2_g), rows(ln2_b)
    ffn_conv_b, ple_b_gate = rows(ffn_conv_b), rows(ple_b_gate)
    a_bias = jnp.repeat(jnp.swapaxes(0.5 * a_b_s, 1, 2), A_HEAD_DIM, axis=2)
    for i in range(DEPTH):
        idx = jnp.array([i // 2, i], jnp.int32)
        if i % 2 == 0:
            x = _mixer_a(x, idx, a_w_in, a_ln_g, a_ln_b, a_w_s, a_bias, a_w_o, ln1_g, ln1_b)
        else:
            x = _mixer_b(x, idx, b_w_in, b_w_grp, b_scale, b_w_o, ln1_g, ln1_b)
        x = _ffn(x, idx, p, ffn_w_up, ffn_conv_w, ffn_conv_b, ffn_w_down, ln2_g, ln2_b,
                 ple_w_gate, ple_b_gate, ple_w_p)
    return x
```

```python
import jax
import jax.numpy as jnp
from jax import lax
from jax.experimental import pallas as pl
from jax.experimental.pallas import tpu as pltpu

D_MODEL = 1024
DEPTH = 4
CHUNK = 128
A_HEADS = 8
A_HEAD_DIM = D_MODEL // A_HEADS
POOL_WINDOWS = (2, 4, 8, 16)
B_GROUP_DIM = D_MODEL // len(POOL_WINDOWS)
D_FF = 2816
PLE_DIM = 256
DEEPNORM_ALPHA = (2.0 * DEPTH) ** 0.25
LN_EPS = 1e-5

MIX_ROW_TILE = 2048
MIX_SUB_ROWS = 256
MIX_RING = 3
MIX_NORM_LAG = 2
FFN_ROW_TILE = 512
FFN_TAIL_BLOCKS = 4
SUBLANES = 8
LANES = 128
HALO = SUBLANES
FF_TILE = 256
VMEM_LIMIT_BYTES = 56 * 1024 * 1024
INV_SQRT2 = 0.7071067811865476


def _gelu_x2(x):
    return x * (1.0 + lax.erf(x * INV_SQRT2))


def _layer_norm(x, g, b, eps=LN_EPS):
    mu = jnp.mean(x, axis=-1, keepdims=True)
    xc = x - mu
    var = jnp.mean(xc * xc, axis=-1, keepdims=True)
    return xc * lax.rsqrt(var + eps) * g + b


def _layer_norm_stats(x, g, b):
    mu = jnp.mean(x, axis=-1, keepdims=True)
    xc = x - mu
    var = jnp.mean(xc * xc, axis=-1, keepdims=True)
    rstd = lax.rsqrt(var + LN_EPS)
    return xc * rstd * g + b, rstd


def _zero_after(rstd):
    total = jnp.sum(rstd, axis=0, keepdims=True)
    return jnp.where(total != total, total, 0.0)


def _software_pipeline(n, stages):
    results = [[None] * n for _ in stages]
    for step in range(n + len(stages) - 1):
        for j, stage in enumerate(stages):
            k = step - j
            if 0 <= k < n:
                prev = results[j - 1][k] if j else None
                results[j][k] = prev if stage is None else stage(k, prev)


def _dot(a, b):
    return jnp.dot(a, b, preferred_element_type=jnp.float32)


def _shift_rows(x, k):
    n = x.shape[0]
    return pltpu.roll(x, (n - k) % n, axis=0)


def _halo_rows(prev_ref, next_ref):
    i = pl.program_id(1)
    last = pl.num_programs(1) - 1
    return jnp.where(i > 0, prev_ref[...], 0.0), jnp.where(i < last, next_ref[...], 0.0)


def _layer_spec(shape, slot):
    n = len(shape) - 1
    return pl.BlockSpec((None,) + tuple(shape[1:]), lambda b, i, idx: (idx[slot],) + (0,) * n,
                        pipeline_mode=pl.Buffered(1))


def _row_spec(rows, width):
    return pl.BlockSpec((None, rows, width), lambda b, i, idx: (b, i, 0))


def _halo_specs(rows, seq):
    tiles = rows // HALO
    last = seq // HALO - 1
    prev = pl.BlockSpec((None, HALO, D_MODEL),
                        lambda b, i, idx: (b, jnp.maximum(i * tiles - 1, 0), 0))
    nxt = pl.BlockSpec((None, HALO, D_MODEL),
                       lambda b, i, idx: (b, jnp.minimum((i + 1) * tiles, last), 0))
    return prev, nxt


def _call(body, name, x, rows, in_specs, scratch_shapes, idx, operands):
    bsz, seq, d = x.shape
    grid_spec = pltpu.PrefetchScalarGridSpec(
        num_scalar_prefetch=1, grid=(bsz, seq // rows), in_specs=in_specs,
        out_specs=_row_spec(rows, d), scratch_shapes=scratch_shapes)
    return pl.pallas_call(
        body, grid_spec=grid_spec, out_shape=jax.ShapeDtypeStruct(x.shape, jnp.float32),
        compiler_params=pltpu.CompilerParams(dimension_semantics=("parallel", "arbitrary"),
                                             vmem_limit_bytes=VMEM_LIMIT_BYTES),
        name=name,
    )(idx, *operands)


def _sub_rows(k):
    return slice(k * MIX_SUB_ROWS, (k + 1) * MIX_SUB_ROWS)


def _ring_rows(k):
    return _sub_rows(k % MIX_RING)


def _mixer_a_kernel(idx_ref, x_ref, w_in_ref, ln_g_ref, ln_b_ref, w_s_ref, bias_ref, w_o_ref,
                    ln1_g_ref, ln1_b_ref, o_ref, u_ref, vb_ref, g_ref):
    def project(k, _):
        rows, ring = _sub_rows(k), _ring_rows(k)
        xb = x_ref[rows, :].astype(jnp.bfloat16)
        u_ref[ring, :] = _gelu_x2(_dot(xb, w_in_ref[:, :D_MODEL]))
        v = _gelu_x2(_dot(xb, w_in_ref[:, D_MODEL:]))
        vb_ref[ring, :] = _layer_norm(v, ln_g_ref[...], ln_b_ref[...],
                                      eps=4.0 * LN_EPS).astype(jnp.bfloat16)

    normed = {}

    def mix(k, _):
        ring = _ring_rows(k)
        tie = normed.get(k - MIX_NORM_LAG, 0.0)
        for r0 in range(ring.start, ring.stop, CHUNK):
            chunk = slice(r0, r0 + CHUNK)
            for h in range(A_HEADS):
                cols = slice(h * A_HEAD_DIM, (h + 1) * A_HEAD_DIM)
                s = _dot(w_s_ref[h], vb_ref[chunk, cols]) + (bias_ref[:, cols] + tie)
                g_ref[chunk, cols] = (u_ref[chunk, cols] * s).astype(jnp.bfloat16)
        return _dot(g_ref[ring, :], w_o_ref[...])

    def finish(k, mixed):
        rows = _sub_rows(k)
        o_ref[rows, :], rstd = _layer_norm_stats(DEEPNORM_ALPHA * x_ref[rows, :] + mixed,
                                                 ln1_g_ref[...], ln1_b_ref[...])
        normed[k] = _zero_after(rstd)

    _software_pipeline(MIX_ROW_TILE // MIX_SUB_ROWS, [project, None, mix, finish])


def _mixer_a(x, idx, w_in, ln_g, ln_b, w_s, bias, w_o, ln1_g, ln1_b):
    d = x.shape[-1]
    in_specs = [_row_spec(MIX_ROW_TILE, d), _layer_spec(w_in.shape, 0),
                _layer_spec(ln_g.shape, 0), _layer_spec(ln_b.shape, 0),
                _layer_spec(w_s.shape, 0), _layer_spec(bias.shape, 0), _layer_spec(w_o.shape, 0),
                _layer_spec(ln1_g.shape, 1), _layer_spec(ln1_b.shape, 1)]
    ring = MIX_RING * MIX_SUB_ROWS
    scratch = [pltpu.VMEM((ring, d), jnp.float32), pltpu.VMEM((ring, d), jnp.bfloat16),
               pltpu.VMEM((ring, d), jnp.bfloat16)]
    return _call(_mixer_a_kernel, "mixer_a", x, MIX_ROW_TILE, in_specs, scratch, idx,
                 (x, w_in, ln_g, ln_b, w_s, bias, w_o, ln1_g, ln1_b))


def _mixer_b_kernel(idx_ref, x_ref, w_in_ref, w_grp_ref, scale_ref, w_o_ref, ln1_g_ref,
                    ln1_b_ref, o_ref, pooled_ref, y_ref):
    sub, seq = MIX_SUB_ROWS, MIX_ROW_TILE
    edge = jnp.zeros((HALO, D_MODEL), jnp.float32)

    def project(k, _):
        rows = _sub_rows(k)
        before = x_ref[rows.start - HALO:rows.start, :] if rows.start else edge
        after = x_ref[rows.stop:rows.stop + HALO, :] if rows.stop < seq else edge
        xh = jnp.concatenate([before, x_ref[rows, :], after], axis=0)
        h = _dot(xh.astype(jnp.bfloat16), w_in_ref[...])
        t = rows.start + lax.broadcasted_iota(jnp.int32, (sub, 1), 0)
        for g, window in enumerate(POOL_WINDOWS):
            cols = slice(g * B_GROUP_DIM, (g + 1) * B_GROUP_DIM)
            hg = h[:, cols]
            run, span = hg, 1
            while span < window:
                run = run + _shift_rows(run, span)
                span *= 2
            half = window // 2
            total = run[HALO - half:HALO - half + sub, :]
            lo = jnp.maximum(t - half, 0)
            hi = jnp.minimum(t + (window - half - 1), seq - 1)
            count = (hi - lo + 1).astype(jnp.float32)
            pooled = total / count - hg[HALO:HALO + sub, :]
            pooled_ref[_ring_rows(k), cols] = pooled.astype(jnp.bfloat16)

    normed = {}

    def mix(k, _):
        ring = _ring_rows(k)
        tie = normed.get(k - MIX_NORM_LAG, 0.0)
        for g in range(len(POOL_WINDOWS)):
            cols = slice(g * B_GROUP_DIM, (g + 1) * B_GROUP_DIM)
            y = _dot(pooled_ref[ring, cols], w_grp_ref[g])
            y_ref[ring, cols] = (y * (scale_ref[:, cols] + tie)).astype(jnp.bfloat16)
        return _dot(y_ref[ring, :], w_o_ref[...])

    def finish(k, mixed):
        rows = _sub_rows(k)
        o_ref[rows, :], rstd = _layer_norm_stats(DEEPNORM_ALPHA * x_ref[rows, :] + mixed,
                                                 ln1_g_ref[...], ln1_b_ref[...])
        normed[k] = _zero_after(rstd)

    _software_pipeline(seq // sub, [project, None, mix, finish])


def _mixer_b(x, idx, w_in, w_grp, scale, w_o, ln1_g, ln1_b):
    seq, d = x.shape[1:]
    assert seq == MIX_ROW_TILE
    in_specs = [_row_spec(MIX_ROW_TILE, d), _layer_spec(w_in.shape, 0),
                _layer_spec(w_grp.shape, 0), _layer_spec(scale.shape, 0),
                _layer_spec(w_o.shape, 0), _layer_spec(ln1_g.shape, 1),
                _layer_spec(ln1_b.shape, 1)]
    ring = MIX_RING * MIX_SUB_ROWS
    scratch = [pltpu.VMEM((ring, d), jnp.bfloat16), pltpu.VMEM((ring, d), jnp.bfloat16)]
    return _call(_mixer_b_kernel, "mixer_b", x, MIX_ROW_TILE, in_specs, scratch, idx,
                 (x, w_in, w_grp, scale, w_o, ln1_g, ln1_b))


def _ffn_kernel(idx_ref, prev_ref, x_ref, next_ref, p_ref, w_up_ref, conv_w_ref, conv_b_ref,
                w_down_ref, ln2_g_ref, ln2_b_ref, w_gate_ref, b_gate_ref, w_p_ref, o_ref,
                xs_ref, ps_ref, os_ref, xb_ref, a_ref):
    rows = FFN_ROW_TILE
    ext = rows + 2 * HALO
    groups = ext // SUBLANES
    prev, nxt = _halo_rows(prev_ref, next_ref)
    for c in range(D_MODEL // LANES):
        lanes = slice(c * LANES, (c + 1) * LANES)
        xs_ref[c, 0:HALO, :] = prev[:, lanes]
        xs_ref[c, HALO:HALO + rows, :] = x_ref[:, lanes]
        xs_ref[c, HALO + rows:, :] = nxt[:, lanes]
    zeros = jnp.zeros((HALO, LANES), jnp.float32)
    for c in range(PLE_DIM // LANES):
        lanes = slice(c * LANES, (c + 1) * LANES)
        ps_ref[c, 0:HALO, :] = zeros
        ps_ref[c, HALO:HALO + rows, :] = p_ref[:, lanes]
        ps_ref[c, HALO + rows:, :] = zeros

    def seq_rows(ref, i0, n, width):
        return jnp.concatenate(
            [jnp.concatenate([ref[c, pl.ds(i, SUBLANES, stride=groups), :]
                              for c in range(width // LANES)], axis=1)
             for i in range(i0, i0 + n)], axis=0)

    for i in range(0, groups, 2):
        xb_ref[i * SUBLANES:(i + 2) * SUBLANES, :] = seq_rows(xs_ref, i, 2, D_MODEL).astype(
            jnp.bfloat16)

    def up_conv(col0):
        cols = slice(col0, col0 + FF_TILE)
        h = _dot(xb_ref[...], w_up_ref[:, cols])
        last = ext - SUBLANES
        below = jnp.concatenate([pltpu.roll(h[last:, :], 1, axis=0), h[:last, :]], axis=0)
        above = jnp.concatenate([h[SUBLANES:, :], pltpu.roll(h[:SUBLANES, :], SUBLANES - 1, axis=0)],
                                axis=0)
        return (conv_w_ref[0:1, cols] * below + conv_w_ref[1:2, cols] * h
                + conv_w_ref[2:3, cols] * above + conv_b_ref[:, cols])

    for j in range(D_FF // FF_TILE):
        gate = up_conv(j * FF_TILE)
        val = up_conv(D_FF + j * FF_TILE)
        a_ref[:, j * FF_TILE:(j + 1) * FF_TILE] = (_gelu_x2(gate) * val).astype(jnp.bfloat16)

    edges = [(ext * k // FFN_TAIL_BLOCKS + 15) // 16 * 16 for k in range(FFN_TAIL_BLOCKS)] + [ext]
    blocks = [slice(a, b) for a, b in zip(edges[:-1], edges[1:])]

    def down_project(k, _):
        blk = blocks[k]
        g0, n = blk.start // SUBLANES, (blk.stop - blk.start) // SUBLANES
        y = (DEEPNORM_ALPHA * seq_rows(xs_ref, g0, n, D_MODEL)
             + _dot(a_ref[blk, :], w_down_ref[...]))
        emb = _dot(seq_rows(ps_ref, g0, n, PLE_DIM).astype(jnp.bfloat16), w_p_ref[...])
        return y, emb

    def norm_gate(k, y_emb):
        y, emb = y_emb
        x2 = _layer_norm(y, ln2_g_ref[...], ln2_b_ref[...])
        pre = _dot(x2.astype(jnp.bfloat16), w_gate_ref[...]) + b_gate_ref[...]
        return x2, pre, emb

    def embed(k, parts):
        x2, pre, emb = parts
        out = x2 + jax.nn.sigmoid(pre) * emb
        g0 = blocks[k].start // SUBLANES
        for i in range(out.shape[0] // SUBLANES):
            for c in range(D_MODEL // LANES):
                os_ref[c, pl.ds(g0 + i, SUBLANES, stride=groups), :] = out[
                    i * SUBLANES:(i + 1) * SUBLANES, c * LANES:(c + 1) * LANES]

    _software_pipeline(len(blocks), [down_project, norm_gate, embed])
    for c in range(D_MODEL // LANES):
        o_ref[:, c * LANES:(c + 1) * LANES] = os_ref[c, HALO:HALO + rows, :]


def _ffn(x, idx, p, w_up, conv_w, conv_b, w_down, ln2_g, ln2_b, w_gate, b_gate, w_p):
    seq, d = x.shape[1:]
    rows = FFN_ROW_TILE
    ext = rows + 2 * HALO
    prev, nxt = _halo_specs(rows, seq)
    p_spec = pl.BlockSpec((None, None, rows, PLE_DIM), lambda b, i, idx: (idx[1], b, i, 0))
    in_specs = [prev, _row_spec(rows, d), nxt, p_spec, _layer_spec(w_up.shape, 1),
                _layer_spec(conv_w.shape, 1), _layer_spec(conv_b.shape, 1),
                _layer_spec(w_down.shape, 1), _layer_spec(ln2_g.shape, 1),
                _layer_spec(ln2_b.shape, 1), _layer_spec(w_gate.shape, 1),
                _layer_spec(b_gate.shape, 1), _layer_spec(w_p.shape, 1)]
    scratch = [pltpu.VMEM((d // LANES, ext, LANES), jnp.float32),
               pltpu.VMEM((PLE_DIM // LANES, ext, LANES), jnp.float32),
               pltpu.VMEM((d // LANES, ext, LANES), jnp.float32),
               pltpu.VMEM((ext, d), jnp.bfloat16),
               pltpu.VMEM((ext, D_FF), jnp.bfloat16)]
    return _call(_ffn_kernel, "conv_ffn", x, rows, in_specs, scratch, idx,
                 (x, x, x, p, w_up, conv_w, conv_b, w_down, ln2_g, ln2_b, w_gate, b_gate, w_p))


def kernel(x, p, a_w_in, a_ln_g, a_ln_b, a_w_s, a_b_s, a_w_o, b_w_in, b_w_grp, b_scale, b_w_o,
           ffn_w_up, ffn_conv_w, ffn_conv_b, ffn_w_down, ln1_g, ln1_b, ln2_g, ln2_b,
           ple_w_p, ple_w_gate, ple_b_gate):
    bf16 = lambda w: w.astype(jnp.bfloat16)
    rows = lambda v: v[:, None, :]
    a_w_in, a_w_s, a_w_o = bf16(a_w_in), bf16(0.5 * a_w_s), bf16(a_w_o)
    b_w_in, b_w_grp, b_w_o = bf16(b_w_in), bf16(b_w_grp), bf16(b_w_o)
    ffn_w_up, ffn_w_down = bf16(ffn_w_up), bf16(0.5 * ffn_w_down)
    ple_w_gate, ple_w_p = bf16(ple_w_gate), bf16(ple_w_p)
    a_ln_g, a_ln_b, b_scale = rows(a_ln_g), rows(a_ln_b), rows(b_scale)
    ln1_g, ln1_b, ln2_g, ln2_b = rows(ln1_g), rows(ln1_b), rows(ln2_g), rows(ln2_b)
    ffn_conv_b, ple_b_gate = rows(ffn_conv_b), rows(ple_b_gate)
    a_bias = jnp.repeat(jnp.swapaxes(0.5 * a_b_s, 1, 2), A_HEAD_DIM, axis=2)
    for i in range(DEPTH):
        idx = jnp.array([i // 2, i], jnp.int32)
        if i % 2 == 0:
            x = _mixer_a(x, idx, a_w_in, a_ln_g, a_ln_b, a_w_s, a_bias, a_w_o, ln1_g, ln1_b)
        else:
            x = _mixer_b(x, idx, b_w_in, b_w_grp, b_scale, b_w_o, ln1_g, ln1_b)
        x = _ffn(x, idx, p, ffn_w_up, ffn_conv_w, ffn_conv_b, ffn_w_down, ln2_g, ln2_b,
                 ple_w_gate, ple_b_gate, ple_w_p)
    return x
```

```python
import jax
import jax.numpy as jnp
from jax import lax
from jax.experimental import pallas as pl
from jax.experimental.pallas import tpu as pltpu

D_MODEL = 1024
DEPTH = 4
CHUNK = 128
A_HEADS = 8
A_HEAD_DIM = D_MODEL // A_HEADS
POOL_WINDOWS = (2, 4, 8, 16)
B_GROUP_DIM = D_MODEL // len(POOL_WINDOWS)
D_FF = 2816
PLE_DIM = 256
DEEPNORM_ALPHA = (2.0 * DEPTH) ** 0.25
LN_EPS = 1e-5

MIX_ROW_TILE = 2048
MIX_SUB_ROWS = 256
MIX_RING = 3
MIX_NORM_LAG = 2
FFN_ROW_TILE = 512
FFN_TAIL_BLOCKS = 2
SUBLANES = 8
LANES = 128
HALO = SUBLANES
FF_TILE = 256
VMEM_LIMIT_BYTES = 56 * 1024 * 1024
INV_SQRT2 = 0.7071067811865476


def _gelu_x2(x):
    return x * (1.0 + lax.erf(x * INV_SQRT2))


def _layer_norm(x, g, b, eps=LN_EPS):
    mu = jnp.mean(x, axis=-1, keepdims=True)
    xc = x - mu
    var = jnp.mean(xc * xc, axis=-1, keepdims=True)
    return xc * lax.rsqrt(var + eps) * g + b


def _layer_norm_stats(x, g, b):
    mu = jnp.mean(x, axis=-1, keepdims=True)
    xc = x - mu
    var = jnp.mean(xc * xc, axis=-1, keepdims=True)
    rstd = lax.rsqrt(var + LN_EPS)
    return xc * rstd * g + b, rstd


def _zero_after(rstd):
    total = jnp.sum(rstd, axis=0, keepdims=True)
    return jnp.where(total != total, total, 0.0)


def _software_pipeline(n, stages):
    results = [[None] * n for _ in stages]
    for step in range(n + len(stages) - 1):
        for j, stage in enumerate(stages):
            k = step - j
            if 0 <= k < n:
                prev = results[j - 1][k] if j else None
                results[j][k] = prev if stage is None else stage(k, prev)


def _dot(a, b):
    return jnp.dot(a, b, preferred_element_type=jnp.float32)


def _shift_rows(x, k):
    n = x.shape[0]
    return pltpu.roll(x, (n - k) % n, axis=0)


def _halo_rows(prev_ref, next_ref):
    i = pl.program_id(1)
    last = pl.num_programs(1) - 1
    return jnp.where(i > 0, prev_ref[...], 0.0), jnp.where(i < last, next_ref[...], 0.0)


def _layer_spec(shape, slot):
    n = len(shape) - 1
    return pl.BlockSpec((None,) + tuple(shape[1:]), lambda b, i, idx: (idx[slot],) + (0,) * n,
                        pipeline_mode=pl.Buffered(1))


def _row_spec(rows, width):
    return pl.BlockSpec((None, rows, width), lambda b, i, idx: (b, i, 0))


def _halo_specs(rows, seq):
    tiles = rows // HALO
    last = seq // HALO - 1
    prev = pl.BlockSpec((None, HALO, D_MODEL),
                        lambda b, i, idx: (b, jnp.maximum(i * tiles - 1, 0), 0))
    nxt = pl.BlockSpec((None, HALO, D_MODEL),
                       lambda b, i, idx: (b, jnp.minimum((i + 1) * tiles, last), 0))
    return prev, nxt


def _call(body, name, x, rows, in_specs, scratch_shapes, idx, operands):
    bsz, seq, d = x.shape
    grid_spec = pltpu.PrefetchScalarGridSpec(
        num_scalar_prefetch=1, grid=(bsz, seq // rows), in_specs=in_specs,
        out_specs=_row_spec(rows, d), scratch_shapes=scratch_shapes)
    return pl.pallas_call(
        body, grid_spec=grid_spec, out_shape=jax.ShapeDtypeStruct(x.shape, jnp.float32),
        compiler_params=pltpu.CompilerParams(dimension_semantics=("parallel", "arbitrary"),
                                             vmem_limit_bytes=VMEM_LIMIT_BYTES),
        name=name,
    )(idx, *operands)


def _sub_rows(k):
    return slice(k * MIX_SUB_ROWS, (k + 1) * MIX_SUB_ROWS)


def _ring_rows(k):
    return _sub_rows(k % MIX_RING)


def _mixer_a_kernel(idx_ref, x_ref, w_in_ref, ln_g_ref, ln_b_ref, w_s_ref, bias_ref, w_o_ref,
                    ln1_g_ref, ln1_b_ref, o_ref, u_ref, vb_ref, g_ref):
    def project(k, _):
        rows, ring = _sub_rows(k), _ring_rows(k)
        xb = x_ref[rows, :].astype(jnp.bfloat16)
        u_ref[ring, :] = _gelu_x2(_dot(xb, w_in_ref[:, :D_MODEL]))
        v = _gelu_x2(_dot(xb, w_in_ref[:, D_MODEL:]))
        vb_ref[ring, :] = _layer_norm(v, ln_g_ref[...], ln_b_ref[...],
                                      eps=4.0 * LN_EPS).astype(jnp.bfloat16)

    normed = {}

    def mix(k, _):
        ring = _ring_rows(k)
        tie = normed.get(k - MIX_NORM_LAG, 0.0)
        for r0 in range(ring.start, ring.stop, CHUNK):
            chunk = slice(r0, r0 + CHUNK)
            for h in range(A_HEADS):
                cols = slice(h * A_HEAD_DIM, (h + 1) * A_HEAD_DIM)
                s = _dot(w_s_ref[h], vb_ref[chunk, cols]) + (bias_ref[:, cols] + tie)
                g_ref[chunk, cols] = (u_ref[chunk, cols] * s).astype(jnp.bfloat16)
        return _dot(g_ref[ring, :], w_o_ref[...])

    def finish(k, mixed):
        rows = _sub_rows(k)
        o_ref[rows, :], rstd = _layer_norm_stats(DEEPNORM_ALPHA * x_ref[rows, :] + mixed,
                                                 ln1_g_ref[...], ln1_b_ref[...])
        normed[k] = _zero_after(rstd)

    _software_pipeline(MIX_ROW_TILE // MIX_SUB_ROWS, [project, None, mix, finish])


def _mixer_a(x, idx, w_in, ln_g, ln_b, w_s, bias, w_o, ln1_g, ln1_b):
    d = x.shape[-1]
    in_specs = [_row_spec(MIX_ROW_TILE, d), _layer_spec(w_in.shape, 0),
                _layer_spec(ln_g.shape, 0), _layer_spec(ln_b.shape, 0),
                _layer_spec(w_s.shape, 0), _layer_spec(bias.shape, 0), _layer_spec(w_o.shape, 0),
                _layer_spec(ln1_g.shape, 1), _layer_spec(ln1_b.shape, 1)]
    ring = MIX_RING * MIX_SUB_ROWS
    scratch = [pltpu.VMEM((ring, d), jnp.float32), pltpu.VMEM((ring, d), jnp.bfloat16),
               pltpu.VMEM((ring, d), jnp.bfloat16)]
    return _call(_mixer_a_kernel, "mixer_a", x, MIX_ROW_TILE, in_specs, scratch, idx,
                 (x, w_in, ln_g, ln_b, w_s, bias, w_o, ln1_g, ln1_b))


def _mixer_b_kernel(idx_ref, x_ref, w_in_ref, w_grp_ref, scale_ref, w_o_ref, ln1_g_ref,
                    ln1_b_ref, o_ref, pooled_ref, y_ref):
    sub, seq = MIX_SUB_ROWS, MIX_ROW_TILE
    edge = jnp.zeros((HALO, D_MODEL), jnp.float32)

    def project(k, _):
        rows = _sub_rows(k)
        before = x_ref[rows.start - HALO:rows.start, :] if rows.start else edge
        after = x_ref[rows.stop:rows.stop + HALO, :] if rows.stop < seq else edge
        xh = jnp.concatenate([before, x_ref[rows, :], after], axis=0)
        h = _dot(xh.astype(jnp.bfloat16), w_in_ref[...])
        t = rows.start + lax.broadcasted_iota(jnp.int32, (sub, 1), 0)
        for g, window in enumerate(POOL_WINDOWS):
            cols = slice(g * B_GROUP_DIM, (g + 1) * B_GROUP_DIM)
            hg = h[:, cols]
            run, span = hg, 1
            while span < window:
                run = run + _shift_rows(run, span)
                span *= 2
            half = window // 2
            total = run[HALO - half:HALO - half + sub, :]
            lo = jnp.maximum(t - half, 0)
            hi = jnp.minimum(t + (window - half - 1), seq - 1)
            count = (hi - lo + 1).astype(jnp.float32)
            pooled = total / count - hg[HALO:HALO + sub, :]
            pooled_ref[_ring_rows(k), cols] = pooled.astype(jnp.bfloat16)

    normed = {}

    def mix(k, _):
        ring = _ring_rows(k)
        tie = normed.get(k - MIX_NORM_LAG, 0.0)
        for g in range(len(POOL_WINDOWS)):
            cols = slice(g * B_GROUP_DIM, (g + 1) * B_GROUP_DIM)
            y = _dot(pooled_ref[ring, cols], w_grp_ref[g])
            y_ref[ring, cols] = (y * (scale_ref[:, cols] + tie)).astype(jnp.bfloat16)
        return _dot(y_ref[ring, :], w_o_ref[...])

    def finish(k, mixed):
        rows = _sub_rows(k)
        o_ref[rows, :], rstd = _layer_norm_stats(DEEPNORM_ALPHA * x_ref[rows, :] + mixed,
                                                 ln1_g_ref[...], ln1_b_ref[...])
        normed[k] = _zero_after(rstd)

    _software_pipeline(seq // sub, [project, None, mix, finish])


def _mixer_b(x, idx, w_in, w_grp, scale, w_o, ln1_g, ln1_b):
    seq, d = x.shape[1:]
    assert seq == MIX_ROW_TILE
    in_specs = [_row_spec(MIX_ROW_TILE, d), _layer_spec(w_in.shape, 0),
                _layer_spec(w_grp.shape, 0), _layer_spec(scale.shape, 0),
                _layer_spec(w_o.shape, 0), _layer_spec(ln1_g.shape, 1),
                _layer_spec(ln1_b.shape, 1)]
    ring = MIX_RING * MIX_SUB_ROWS
    scratch = [pltpu.VMEM((ring, d), jnp.bfloat16), pltpu.VMEM((ring, d), jnp.bfloat16)]
    return _call(_mixer_b_kernel, "mixer_b", x, MIX_ROW_TILE, in_specs, scratch, idx,
                 (x, w_in, w_grp, scale, w_o, ln1_g, ln1_b))


def _ffn_kernel(idx_ref, prev_ref, x_ref, next_ref, p_ref, w_up_ref, conv_w_ref, conv_b_ref,
                w_down_ref, ln2_g_ref, ln2_b_ref, w_gate_ref, b_gate_ref, w_p_ref, o_ref,
                xs_ref, ys_ref, xb_ref, a_ref):
    rows = FFN_ROW_TILE
    ext = rows + 2 * HALO
    groups = ext // SUBLANES
    slabs = D_MODEL // LANES
    prev, nxt = _halo_rows(prev_ref, next_ref)
    for c in range(slabs):
        lanes = slice(c * LANES, (c + 1) * LANES)
        xs_ref[c, 0:HALO, :] = prev[:, lanes]
        xs_ref[c, HALO:HALO + rows, :] = x_ref[:, lanes]
        xs_ref[c, HALO + rows:, :] = nxt[:, lanes]
    for i in range(0, groups, 2):
        xb_ref[i * SUBLANES:(i + 2) * SUBLANES, :] = jnp.concatenate(
            [jnp.concatenate([xs_ref[c, pl.ds(j, SUBLANES, stride=groups), :]
                              for c in range(slabs)], axis=1)
             for j in (i, i + 1)], axis=0).astype(jnp.bfloat16)

    def up_conv(col0):
        cols = slice(col0, col0 + FF_TILE)
        h = _dot(xb_ref[...], w_up_ref[:, cols])
        last = ext - SUBLANES
        below = jnp.concatenate([pltpu.roll(h[last:, :], 1, axis=0), h[:last, :]], axis=0)
        above = jnp.concatenate([h[SUBLANES:, :], pltpu.roll(h[:SUBLANES, :], SUBLANES - 1, axis=0)],
                                axis=0)
        return (conv_w_ref[0:1, cols] * below + conv_w_ref[1:2, cols] * h
                + conv_w_ref[2:3, cols] * above + conv_b_ref[:, cols])

    for j in range(D_FF // FF_TILE):
        gate = up_conv(j * FF_TILE)
        val = up_conv(D_FF + j * FF_TILE)
        a_ref[:, j * FF_TILE:(j + 1) * FF_TILE] = (_gelu_x2(gate) * val).astype(jnp.bfloat16)

    edges = [(ext * k // FFN_TAIL_BLOCKS + 15) // 16 * 16 for k in range(FFN_TAIL_BLOCKS)] + [ext]
    for lo, hi in zip(edges[:-1], edges[1:]):
        ffn = _dot(a_ref[lo:hi, :], w_down_ref[...])
        for i in range(lo // SUBLANES, hi // SUBLANES):
            at = slice(i * SUBLANES - lo, (i + 1) * SUBLANES - lo)
            for c in range(slabs):
                ys_ref[c, pl.ds(i, SUBLANES, stride=groups), :] = ffn[at, c * LANES:(c + 1) * LANES]

    emb = _dot(p_ref[...].astype(jnp.bfloat16), w_p_ref[...])
    half = rows // FFN_TAIL_BLOCKS

    def norm_gate(k, _):
        blk = slice(k * half, (k + 1) * half)
        ffn = jnp.concatenate([ys_ref[c, HALO + blk.start:HALO + blk.stop, :]
                               for c in range(slabs)], axis=1)
        x2 = _layer_norm(DEEPNORM_ALPHA * x_ref[blk, :] + ffn, ln2_g_ref[...], ln2_b_ref[...])
        return x2, _dot(x2.astype(jnp.bfloat16), w_gate_ref[...]) + b_gate_ref[...]

    def embed(k, parts):
        x2, pre = parts
        blk = slice(k * half, (k + 1) * half)
        o_ref[blk, :] = x2 + jax.nn.sigmoid(pre) * emb[blk, :]

    _software_pipeline(FFN_TAIL_BLOCKS, [norm_gate, embed])


def _ffn(x, idx, p, w_up, conv_w, conv_b, w_down, ln2_g, ln2_b, w_gate, b_gate, w_p):
    seq, d = x.shape[1:]
    rows = FFN_ROW_TILE
    ext = rows + 2 * HALO
    prev, nxt = _halo_specs(rows, seq)
    p_spec = pl.BlockSpec((None, None, rows, PLE_DIM), lambda b, i, idx: (idx[1], b, i, 0))
    in_specs = [prev, _row_spec(rows, d), nxt, p_spec, _layer_spec(w_up.shape, 1),
                _layer_spec(conv_w.shape, 1), _layer_spec(conv_b.shape, 1),
                _layer_spec(w_down.shape, 1), _layer_spec(ln2_g.shape, 1),
                _layer_spec(ln2_b.shape, 1), _layer_spec(w_gate.shape, 1),
                _layer_spec(b_gate.shape, 1), _layer_spec(w_p.shape, 1)]
    scratch = [pltpu.VMEM((d // LANES, ext, LANES), jnp.float32),
               pltpu.VMEM((d // LANES, ext, LANES), jnp.float32),
               pltpu.VMEM((ext, d), jnp.bfloat16),
               pltpu.VMEM((ext, D_FF), jnp.bfloat16)]
    return _call(_ffn_kernel, "conv_ffn", x, rows, in_specs, scratch, idx,
                 (x, x, x, p, w_up, conv_w, conv_b, w_down, ln2_g, ln2_b, w_gate, b_gate, w_p))


def kernel(x, p, a_w_in, a_ln_g, a_ln_b, a_w_s, a_b_s, a_w_o, b_w_in, b_w_grp, b_scale, b_w_o,
           ffn_w_up, ffn_conv_w, ffn_conv_b, ffn_w_down, ln1_g, ln1_b, ln2_g, ln2_b,
           ple_w_p, ple_w_gate, ple_b_gate):
    bf16 = lambda w: w.astype(jnp.bfloat16)
    rows = lambda v: v[:, None, :]
    a_w_in, a_w_s, a_w_o = bf16(a_w_in), bf16(0.5 * a_w_s), bf16(a_w_o)
    b_w_in, b_w_grp, b_w_o = bf16(b_w_in), bf16(b_w_grp), bf16(b_w_o)
    ffn_w_up, ffn_w_down = bf16(ffn_w_up), bf16(0.5 * ffn_w_down)
    ple_w_gate, ple_w_p = bf16(ple_w_gate), bf16(ple_w_p)
    a_ln_g, a_ln_b, b_scale = rows(a_ln_g), rows(a_ln_b), rows(b_scale)
    ln1_g, ln1_b, ln2_g, ln2_b = rows(ln1_g), rows(ln1_b), rows(ln2_g), rows(ln2_b)
    ffn_conv_b, ple_b_gate = rows(ffn_conv_b), rows(ple_b_gate)
    a_bias = jnp.repeat(jnp.swapaxes(0.5 * a_b_s, 1, 2), A_HEAD_DIM, axis=2)
    for i in range(DEPTH):
        idx = jnp.array([i // 2, i], jnp.int32)
        if i % 2 == 0:
            x = _mixer_a(x, idx, a_w_in, a_ln_g, a_ln_b, a_w_s, a_bias, a_w_o, ln1_g, ln1_b)
        else:
            x = _mixer_b(x, idx, b_w_in, b_w_grp, b_scale, b_w_o, ln1_g, ln1_b)
        x = _ffn(x, idx, p, ffn_w_up, ffn_conv_w, ffn_conv_b, ffn_w_down, ln2_g, ln2_b,
                 ple_w_gate, ple_b_gate, ple_w_p)
    return x
```

```python
import jax
import jax.numpy as jnp
from jax import lax
from jax.experimental import pallas as pl
from jax.experimental.pallas import tpu as pltpu

D_MODEL = 1024
DEPTH = 4
CHUNK = 128
A_HEADS = 8
A_HEAD_DIM = D_MODEL // A_HEADS
POOL_WINDOWS = (2, 4, 8, 16)
B_GROUP_DIM = D_MODEL // len(POOL_WINDOWS)
D_FF = 2816
PLE_DIM = 256
DEEPNORM_ALPHA = (2.0 * DEPTH) ** 0.25
LN_EPS = 1e-5

MIX_ROW_TILE = 2048
MIX_SUB_ROWS = 256
MIX_RING = 3
MIX_NORM_LAG = 2
FFN_ROW_TILE = 512
FFN_TAIL_BLOCKS = 2
SUBLANES = 8
LANES = 128
HALO = SUBLANES
FF_TILE = 256
VMEM_LIMIT_BYTES = 56 * 1024 * 1024
INV_SQRT2 = 0.7071067811865476


def _gelu_x2(x):
    return x * (1.0 + lax.erf(x * INV_SQRT2))


def _layer_norm(x, g, b, eps=LN_EPS):
    mu = jnp.mean(x, axis=-1, keepdims=True)
    xc = x - mu
    var = jnp.mean(xc * xc, axis=-1, keepdims=True)
    return xc * lax.rsqrt(var + eps) * g + b


def _layer_norm_stats(x, g, b):
    mu = jnp.mean(x, axis=-1, keepdims=True)
    xc = x - mu
    var = jnp.mean(xc * xc, axis=-1, keepdims=True)
    rstd = lax.rsqrt(var + LN_EPS)
    return xc * rstd * g + b, rstd


def _zero_after(rstd):
    total = jnp.sum(rstd, axis=0, keepdims=True)
    return jnp.where(total != total, total, 0.0)


def _software_pipeline(n, stages):
    results = [[None] * n for _ in stages]
    for step in range(n + len(stages) - 1):
        for j, stage in enumerate(stages):
            k = step - j
            if 0 <= k < n:
                prev = results[j - 1][k] if j else None
                results[j][k] = prev if stage is None else stage(k, prev)


def _dot(a, b):
    return jnp.dot(a, b, preferred_element_type=jnp.float32)


def _shift_rows(x, k):
    n = x.shape[0]
    return pltpu.roll(x, (n - k) % n, axis=0)


def _halo_rows(prev_ref, next_ref):
    i = pl.program_id(1)
    last = pl.num_programs(1) - 1
    return jnp.where(i > 0, prev_ref[...], 0.0), jnp.where(i < last, next_ref[...], 0.0)


def _layer_spec(shape, slot):
    n = len(shape) - 1
    return pl.BlockSpec((None,) + tuple(shape[1:]), lambda b, i, idx: (idx[slot],) + (0,) * n,
                        pipeline_mode=pl.Buffered(1))


def _row_spec(rows, width):
    return pl.BlockSpec((None, rows, width), lambda b, i, idx: (b, i, 0))


def _halo_specs(rows, seq):
    tiles = rows // HALO
    last = seq // HALO - 1
    prev = pl.BlockSpec((None, HALO, D_MODEL),
                        lambda b, i, idx: (b, jnp.maximum(i * tiles - 1, 0), 0))
    nxt = pl.BlockSpec((None, HALO, D_MODEL),
                       lambda b, i, idx: (b, jnp.minimum((i + 1) * tiles, last), 0))
    return prev, nxt


def _call(body, name, x, rows, in_specs, scratch_shapes, idx, operands):
    bsz, seq, d = x.shape
    grid_spec = pltpu.PrefetchScalarGridSpec(
        num_scalar_prefetch=1, grid=(bsz, seq // rows), in_specs=in_specs,
        out_specs=_row_spec(rows, d), scratch_shapes=scratch_shapes)
    return pl.pallas_call(
        body, grid_spec=grid_spec, out_shape=jax.ShapeDtypeStruct(x.shape, jnp.float32),
        compiler_params=pltpu.CompilerParams(dimension_semantics=("parallel", "arbitrary"),
                                             vmem_limit_bytes=VMEM_LIMIT_BYTES),
        name=name,
    )(idx, *operands)


def _sub_rows(k):
    return slice(k * MIX_SUB_ROWS, (k + 1) * MIX_SUB_ROWS)


def _ring_rows(k):
    return _sub_rows(k % MIX_RING)


def _mixer_a_kernel(idx_ref, x_ref, w_in_ref, ln_g_ref, ln_b_ref, w_s_ref, bias_ref, w_o_ref,
                    ln1_g_ref, ln1_b_ref, o_ref, u_ref, vb_ref, g_ref):
    def project(k, _):
        rows, ring = _sub_rows(k), _ring_rows(k)
        xb = x_ref[rows, :].astype(jnp.bfloat16)
        v = _gelu_x2(_dot(xb, w_in_ref[:, D_MODEL:]))
        vb_ref[ring, :] = _layer_norm(v, ln_g_ref[...], ln_b_ref[...],
                                      eps=4.0 * LN_EPS).astype(jnp.bfloat16)
        u_ref[ring, :] = _gelu_x2(_dot(xb, w_in_ref[:, :D_MODEL]))

    normed = {}

    def mix(k, _):
        ring = _ring_rows(k)
        tie = normed.get(k - MIX_NORM_LAG, 0.0)
        for r0 in range(ring.start, ring.stop, CHUNK):
            chunk = slice(r0, r0 + CHUNK)
            for h in range(A_HEADS):
                cols = slice(h * A_HEAD_DIM, (h + 1) * A_HEAD_DIM)
                s = _dot(w_s_ref[h], vb_ref[chunk, cols]) + (bias_ref[:, cols] + tie)
                g_ref[chunk, cols] = (u_ref[chunk, cols] * s).astype(jnp.bfloat16)
        return _dot(g_ref[ring, :], w_o_ref[...])

    def finish(k, mixed):
        rows = _sub_rows(k)
        o_ref[rows, :], rstd = _layer_norm_stats(DEEPNORM_ALPHA * x_ref[rows, :] + mixed,
                                                 ln1_g_ref[...], ln1_b_ref[...])
        normed[k] = _zero_after(rstd)

    _software_pipeline(MIX_ROW_TILE // MIX_SUB_ROWS, [project, None, mix, finish])


def _mixer_a(x, idx, w_in, ln_g, ln_b, w_s, bias, w_o, ln1_g, ln1_b):
    d = x.shape[-1]
    in_specs = [_row_spec(MIX_ROW_TILE, d), _layer_spec(w_in.shape, 0),
                _layer_spec(ln_g.shape, 0), _layer_spec(ln_b.shape, 0),
                _layer_spec(w_s.shape, 0), _layer_spec(bias.shape, 0), _layer_spec(w_o.shape, 0),
                _layer_spec(ln1_g.shape, 1), _layer_spec(ln1_b.shape, 1)]
    ring = MIX_RING * MIX_SUB_ROWS
    scratch = [pltpu.VMEM((ring, d), jnp.float32), pltpu.VMEM((ring, d), jnp.bfloat16),
               pltpu.VMEM((ring, d), jnp.bfloat16)]
    return _call(_mixer_a_kernel, "mixer_a", x, MIX_ROW_TILE, in_specs, scratch, idx,
                 (x, w_in, ln_g, ln_b, w_s, bias, w_o, ln1_g, ln1_b))


def _mixer_b_kernel(idx_ref, x_ref, w_in_ref, w_grp_ref, scale_ref, w_o_ref, ln1_g_ref,
                    ln1_b_ref, o_ref, pooled_ref, y_ref):
    sub, seq = MIX_SUB_ROWS, MIX_ROW_TILE
    edge = jnp.zeros((HALO, D_MODEL), jnp.float32)

    def project(k, _):
        rows = _sub_rows(k)
        before = x_ref[rows.start - HALO:rows.start, :] if rows.start else edge
        after = x_ref[rows.stop:rows.stop + HALO, :] if rows.stop < seq else edge
        xh = jnp.concatenate([before, x_ref[rows, :], after], axis=0)
        h = _dot(xh.astype(jnp.bfloat16), w_in_ref[...])
        t = rows.start + lax.broadcasted_iota(jnp.int32, (sub, 1), 0)
        for g, window in enumerate(POOL_WINDOWS):
            cols = slice(g * B_GROUP_DIM, (g + 1) * B_GROUP_DIM)
            hg = h[:, cols]
            run, span = hg, 1
            while span < window:
                run = run + _shift_rows(run, span)
                span *= 2
            half = window // 2
            total = run[HALO - half:HALO - half + sub, :]
            lo = jnp.maximum(t - half, 0)
            hi = jnp.minimum(t + (window - half - 1), seq - 1)
            count = (hi - lo + 1).astype(jnp.float32)
            pooled = total / count - hg[HALO:HALO + sub, :]
            pooled_ref[_ring_rows(k), cols] = pooled.astype(jnp.bfloat16)

    normed = {}

    def mix(k, _):
        ring = _ring_rows(k)
        tie = normed.get(k - MIX_NORM_LAG, 0.0)
        for g in range(len(POOL_WINDOWS)):
            cols = slice(g * B_GROUP_DIM, (g + 1) * B_GROUP_DIM)
            y = _dot(pooled_ref[ring, cols], w_grp_ref[g])
            y_ref[ring, cols] = (y * (scale_ref[:, cols] + tie)).astype(jnp.bfloat16)
        return _dot(y_ref[ring, :], w_o_ref[...])

    def finish(k, mixed):
        rows = _sub_rows(k)
        o_ref[rows, :], rstd = _layer_norm_stats(DEEPNORM_ALPHA * x_ref[rows, :] + mixed,
                                                 ln1_g_ref[...], ln1_b_ref[...])
        normed[k] = _zero_after(rstd)

    _software_pipeline(seq // sub, [project, None, mix, finish])


def _mixer_b(x, idx, w_in, w_grp, scale, w_o, ln1_g, ln1_b):
    seq, d = x.shape[1:]
    assert seq == MIX_ROW_TILE
    in_specs = [_row_spec(MIX_ROW_TILE, d), _layer_spec(w_in.shape, 0),
                _layer_spec(w_grp.shape, 0), _layer_spec(scale.shape, 0),
                _layer_spec(w_o.shape, 0), _layer_spec(ln1_g.shape, 1),
                _layer_spec(ln1_b.shape, 1)]
    ring = MIX_RING * MIX_SUB_ROWS
    scratch = [pltpu.VMEM((ring, d), jnp.bfloat16), pltpu.VMEM((ring, d), jnp.bfloat16)]
    return _call(_mixer_b_kernel, "mixer_b", x, MIX_ROW_TILE, in_specs, scratch, idx,
                 (x, w_in, w_grp, scale, w_o, ln1_g, ln1_b))


def _ffn_kernel(idx_ref, prev_ref, x_ref, next_ref, p_ref, w_up_ref, conv_w_ref, conv_b_ref,
                w_down_ref, ln2_g_ref, ln2_b_ref, w_gate_ref, b_gate_ref, w_p_ref, o_ref,
                xs_ref, ys_ref, xb_ref, a_ref):
    rows = FFN_ROW_TILE
    ext = rows + 2 * HALO
    groups = ext // SUBLANES
    slabs = D_MODEL // LANES
    prev, nxt = _halo_rows(prev_ref, next_ref)
    for c in range(slabs):
        lanes = slice(c * LANES, (c + 1) * LANES)
        xs_ref[c, 0:HALO, :] = prev[:, lanes]
        xs_ref[c, HALO:HALO + rows, :] = x_ref[:, lanes]
        xs_ref[c, HALO + rows:, :] = nxt[:, lanes]
    for i in range(0, groups, 2):
        xb_ref[i * SUBLANES:(i + 2) * SUBLANES, :] = jnp.concatenate(
            [jnp.concatenate([xs_ref[c, pl.ds(j, SUBLANES, stride=groups), :]
                              for c in range(slabs)], axis=1)
             for j in (i, i + 1)], axis=0).astype(jnp.bfloat16)

    def up_conv(col0):
        cols = slice(col0, col0 + FF_TILE)
        h = _dot(xb_ref[...], w_up_ref[:, cols])
        last = ext - SUBLANES
        below = jnp.concatenate([pltpu.roll(h[last:, :], 1, axis=0), h[:last, :]], axis=0)
        above = jnp.concatenate([h[SUBLANES:, :], pltpu.roll(h[:SUBLANES, :], SUBLANES - 1, axis=0)],
                                axis=0)
        return (conv_w_ref[0:1, cols] * below + conv_w_ref[1:2, cols] * h
                + conv_w_ref[2:3, cols] * above + conv_b_ref[:, cols])

    for j in range(D_FF // FF_TILE):
        gate = up_conv(j * FF_TILE)
        val = up_conv(D_FF + j * FF_TILE)
        a_ref[:, j * FF_TILE:(j + 1) * FF_TILE] = (_gelu_x2(gate) * val).astype(jnp.bfloat16)

    edges = [(ext * k // FFN_TAIL_BLOCKS + 15) // 16 * 16 for k in range(FFN_TAIL_BLOCKS)] + [ext]
    for lo, hi in zip(edges[:-1], edges[1:]):
        ffn = _dot(a_ref[lo:hi, :], w_down_ref[...])
        for i in range(lo // SUBLANES, hi // SUBLANES):
            at = slice(i * SUBLANES - lo, (i + 1) * SUBLANES - lo)
            for c in range(slabs):
                ys_ref[c, pl.ds(i, SUBLANES, stride=groups), :] = ffn[at, c * LANES:(c + 1) * LANES]

    emb = _dot(p_ref[...].astype(jnp.bfloat16), w_p_ref[...])
    half = rows // FFN_TAIL_BLOCKS

    def norm_gate(k, _):
        blk = slice(k * half, (k + 1) * half)
        ffn = jnp.concatenate([ys_ref[c, HALO + blk.start:HALO + blk.stop, :]
                               for c in range(slabs)], axis=1)
        x2 = _layer_norm(DEEPNORM_ALPHA * x_ref[blk, :] + ffn, ln2_g_ref[...], ln2_b_ref[...])
        return x2, _dot(x2.astype(jnp.bfloat16), w_gate_ref[...]) + b_gate_ref[...]

    def embed(k, parts):
        x2, pre = parts
        blk = slice(k * half, (k + 1) * half)
        o_ref[blk, :] = x2 + jax.nn.sigmoid(pre) * emb[blk, :]

    _software_pipeline(FFN_TAIL_BLOCKS, [norm_gate, embed])


def _ffn(x, idx, p, w_up, conv_w, conv_b, w_down, ln2_g, ln2_b, w_gate, b_gate, w_p):
    seq, d = x.shape[1:]
    rows = FFN_ROW_TILE
    ext = rows + 2 * HALO
    prev, nxt = _halo_specs(rows, seq)
    p_spec = pl.BlockSpec((None, None, rows, PLE_DIM), lambda b, i, idx: (idx[1], b, i, 0))
    in_specs = [prev, _row_spec(rows, d), nxt, p_spec, _layer_spec(w_up.shape, 1),
                _layer_spec(conv_w.shape, 1), _layer_spec(conv_b.shape, 1),
                _layer_spec(w_down.shape, 1), _layer_spec(ln2_g.shape, 1),
                _layer_spec(ln2_b.shape, 1), _layer_spec(w_gate.shape, 1),
                _layer_spec(b_gate.shape, 1), _layer_spec(w_p.shape, 1)]
    scratch = [pltpu.VMEM((d // LANES, ext, LANES), jnp.float32),
               pltpu.VMEM((d // LANES, ext, LANES), jnp.float32),
               pltpu.VMEM((ext, d), jnp.bfloat16),
               pltpu.VMEM((ext, D_FF), jnp.bfloat16)]
    return _call(_ffn_kernel, "conv_ffn", x, rows, in_specs, scratch, idx,
                 (x, x, x, p, w_up, conv_w, conv_b, w_down, ln2_g, ln2_b, w_gate, b_gate, w_p))


def kernel(x, p, a_w_in, a_ln_g, a_ln_b, a_w_s, a_b_s, a_w_o, b_w_in, b_w_grp, b_scale, b_w_o,
           ffn_w_up, ffn_conv_w, ffn_conv_b, ffn_w_down, ln1_g, ln1_b, ln2_g, ln2_b,
           ple_w_p, ple_w_gate, ple_b_gate):
    bf16 = lambda w: w.astype(jnp.bfloat16)
    rows = lambda v: v[:, None, :]
    a_w_in, a_w_s, a_w_o = bf16(a_w_in), bf16(0.5 * a_w_s), bf16(a_w_o)
    b_w_in, b_w_grp, b_w_o = bf16(b_w_in), bf16(b_w_grp), bf16(b_w_o)
    ffn_w_up, ffn_w_down = bf16(ffn_w_up), bf16(0.5 * ffn_w_down)
    ple_w_gate, ple_w_p = bf16(ple_w_gate), bf16(ple_w_p)
    a_ln_g, a_ln_b, b_scale = rows(a_ln_g), rows(a_ln_b), rows(b_scale)
    ln1_g, ln1_b, ln2_g, ln2_b = rows(ln1_g), rows(ln1_b), rows(ln2_g), rows(ln2_b)
    ffn_conv_b, ple_b_gate = rows(ffn_conv_b), rows(ple_b_gate)
    a_bias = jnp.repeat(jnp.swapaxes(0.5 * a_b_s, 1, 2), A_HEAD_DIM, axis=2)
    for i in range(DEPTH):
        idx = jnp.array([i // 2, i], jnp.int32)
        if i % 2 == 0:
            x = _mixer_a(x, idx, a_w_in, a_ln_g, a_ln_b, a_w_s, a_bias, a_w_o, ln1_g, ln1_b)
        else:
            x = _mixer_b(x, idx, b_w_in, b_w_grp, b_scale, b_w_o, ln1_g, ln1_b)
        x = _ffn(x, idx, p, ffn_w_up, ffn_conv_w, ffn_conv_b, ffn_w_down, ln2_g, ln2_b,
                 ple_w_gate, ple_b_gate, ple_w_p)
    return x
```

```python
import jax
import jax.numpy as jnp
from jax import lax
from jax.experimental import pallas as pl
from jax.experimental.pallas import tpu as pltpu

D_MODEL = 1024
DEPTH = 4
CHUNK = 128
A_HEADS = 8
A_HEAD_DIM = D_MODEL // A_HEADS
POOL_WINDOWS = (2, 4, 8, 16)
B_GROUP_DIM = D_MODEL // len(POOL_WINDOWS)
D_FF = 2816
PLE_DIM = 256
DEEPNORM_ALPHA = (2.0 * DEPTH) ** 0.25
LN_EPS = 1e-5

MIX_ROW_TILE = 2048
MIX_SUB_ROWS = 256
MIX_RING = 3
MIX_NORM_LAG = 2
FFN_ROW_TILE = 512
FFN_TAIL_BLOCKS = 2
SUBLANES = 8
LANES = 128
HALO = SUBLANES
FF_TILE = 256
VMEM_LIMIT_BYTES = 56 * 1024 * 1024
INV_SQRT2 = 0.7071067811865476


def _gelu_x2(x):
    return x * (1.0 + lax.erf(x * INV_SQRT2))


def _layer_norm(x, g, b, eps=LN_EPS):
    mu = jnp.mean(x, axis=-1, keepdims=True)
    xc = x - mu
    var = jnp.mean(xc * xc, axis=-1, keepdims=True)
    return xc * lax.rsqrt(var + eps) * g + b


def _layer_norm_stats(x, g, b):
    mu = jnp.mean(x, axis=-1, keepdims=True)
    xc = x - mu
    var = jnp.mean(xc * xc, axis=-1, keepdims=True)
    rstd = lax.rsqrt(var + LN_EPS)
    return xc * rstd * g + b, rstd


def _zero_after(rstd):
    total = jnp.sum(rstd, axis=0, keepdims=True)
    return jnp.where(total != total, total, 0.0)


def _software_pipeline(n, stages):
    results = [[None] * n for _ in stages]
    for step in range(n + len(stages) - 1):
        for j, stage in enumerate(stages):
            k = step - j
            if 0 <= k < n:
                prev = results[j - 1][k] if j else None
                results[j][k] = prev if stage is None else stage(k, prev)


def _dot(a, b):
    return jnp.dot(a, b, preferred_element_type=jnp.float32)


def _shift_rows(x, k):
    n = x.shape[0]
    return pltpu.roll(x, (n - k) % n, axis=0)


def _halo_rows(prev_ref, next_ref):
    i = pl.program_id(1)
    last = pl.num_programs(1) - 1
    return jnp.where(i > 0, prev_ref[...], 0.0), jnp.where(i < last, next_ref[...], 0.0)


def _layer_spec(shape, slot):
    n = len(shape) - 1
    return pl.BlockSpec((None,) + tuple(shape[1:]), lambda b, i, idx: (idx[slot],) + (0,) * n,
                        pipeline_mode=pl.Buffered(1))


def _row_spec(rows, width):
    return pl.BlockSpec((None, rows, width), lambda b, i, idx: (b, i, 0))


def _halo_specs(rows, seq):
    tiles = rows // HALO
    last = seq // HALO - 1
    prev = pl.BlockSpec((None, HALO, D_MODEL),
                        lambda b, i, idx: (b, jnp.maximum(i * tiles - 1, 0), 0))
    nxt = pl.BlockSpec((None, HALO, D_MODEL),
                       lambda b, i, idx: (b, jnp.minimum((i + 1) * tiles, last), 0))
    return prev, nxt


def _call(body, name, x, rows, in_specs, scratch_shapes, idx, operands):
    bsz, seq, d = x.shape
    grid_spec = pltpu.PrefetchScalarGridSpec(
        num_scalar_prefetch=1, grid=(bsz, seq // rows), in_specs=in_specs,
        out_specs=_row_spec(rows, d), scratch_shapes=scratch_shapes)
    return pl.pallas_call(
        body, grid_spec=grid_spec, out_shape=jax.ShapeDtypeStruct(x.shape, jnp.float32),
        compiler_params=pltpu.CompilerParams(dimension_semantics=("parallel", "arbitrary"),
                                             vmem_limit_bytes=VMEM_LIMIT_BYTES),
        name=name,
    )(idx, *operands)


def _sub_rows(k):
    return slice(k * MIX_SUB_ROWS, (k + 1) * MIX_SUB_ROWS)


def _ring_rows(k):
    return _sub_rows(k % MIX_RING)


def _mixer_a_kernel(idx_ref, x_ref, w_in_ref, ln_g_ref, ln_b_ref, w_s_ref, bias_ref, w_o_ref,
                    ln1_g_ref, ln1_b_ref, o_ref, u_ref, vb_ref, g_ref):
    def project(k, _):
        rows, ring = _sub_rows(k), _ring_rows(k)
        xb = x_ref[rows, :].astype(jnp.bfloat16)
        v = _gelu_x2(_dot(xb, w_in_ref[:, D_MODEL:]))
        vb_ref[ring, :] = _layer_norm(v, ln_g_ref[...], ln_b_ref[...],
                                      eps=4.0 * LN_EPS).astype(jnp.bfloat16)
        u_ref[ring, :] = _gelu_x2(_dot(xb, w_in_ref[:, :D_MODEL]))

    normed = {}

    def mix(k, _):
        ring = _ring_rows(k)
        tie = normed.get(k - MIX_NORM_LAG, 0.0)
        for r0 in range(ring.start, ring.stop, CHUNK):
            chunk = slice(r0, r0 + CHUNK)
            for h in range(A_HEADS):
                cols = slice(h * A_HEAD_DIM, (h + 1) * A_HEAD_DIM)
                s = _dot(w_s_ref[h], vb_ref[chunk, cols]) + (bias_ref[:, cols] + tie)
                g_ref[chunk, cols] = (u_ref[chunk, cols] * s).astype(jnp.bfloat16)
        return _dot(g_ref[ring, :], w_o_ref[...])

    def finish(k, mixed):
        rows = _sub_rows(k)
        o_ref[rows, :], rstd = _layer_norm_stats(DEEPNORM_ALPHA * x_ref[rows, :] + mixed,
                                                 ln1_g_ref[...], ln1_b_ref[...])
        normed[k] = _zero_after(rstd)

    _software_pipeline(MIX_ROW_TILE // MIX_SUB_ROWS, [project, None, mix, finish])


def _mixer_a(x, idx, w_in, ln_g, ln_b, w_s, bias, w_o, ln1_g, ln1_b):
    d = x.shape[-1]
    in_specs = [_row_spec(MIX_ROW_TILE, d), _layer_spec(w_in.shape, 0),
                _layer_spec(ln_g.shape, 0), _layer_spec(ln_b.shape, 0),
                _layer_spec(w_s.shape, 0), _layer_spec(bias.shape, 0), _layer_spec(w_o.shape, 0),
                _layer_spec(ln1_g.shape, 1), _layer_spec(ln1_b.shape, 1)]
    ring = MIX_RING * MIX_SUB_ROWS
    scratch = [pltpu.VMEM((ring, d), jnp.float32), pltpu.VMEM((ring, d), jnp.bfloat16),
               pltpu.VMEM((ring, d), jnp.bfloat16)]
    return _call(_mixer_a_kernel, "mixer_a", x, MIX_ROW_TILE, in_specs, scratch, idx,
                 (x, w_in, ln_g, ln_b, w_s, bias, w_o, ln1_g, ln1_b))


def _mixer_b_kernel(idx_ref, x_ref, w_in_ref, w_grp_ref, scale_ref, w_o_ref, ln1_g_ref,
                    ln1_b_ref, o_ref, pooled_ref, y_ref):
    sub, seq = MIX_SUB_ROWS, MIX_ROW_TILE
    edge = jnp.zeros((HALO, D_MODEL), jnp.float32)

    def project(k, _):
        rows = _sub_rows(k)
        before = x_ref[rows.start - HALO:rows.start, :] if rows.start else edge
        after = x_ref[rows.stop:rows.stop + HALO, :] if rows.stop < seq else edge
        xh = jnp.concatenate([before, x_ref[rows, :], after], axis=0)
        h = _dot(xh.astype(jnp.bfloat16), w_in_ref[...])
        t = rows.start + lax.broadcasted_iota(jnp.int32, (sub, 1), 0)
        for g, window in enumerate(POOL_WINDOWS):
            cols = slice(g * B_GROUP_DIM, (g + 1) * B_GROUP_DIM)
            hg = h[:, cols]
            run, span = hg, 1
            while span < window:
                run = run + _shift_rows(run, span)
                span *= 2
            half = window // 2
            total = run[HALO - half:HALO - half + sub, :]
            lo = jnp.maximum(t - half, 0)
            hi = jnp.minimum(t + (window - half - 1), seq - 1)
            count = (hi - lo + 1).astype(jnp.float32)
            pooled = total / count - hg[HALO:HALO + sub, :]
            pooled_ref[_ring_rows(k), cols] = pooled.astype(jnp.bfloat16)

    normed = {}

    def mix(k, _):
        ring = _ring_rows(k)
        tie = normed.get(k - MIX_NORM_LAG, 0.0)
        for g in range(len(POOL_WINDOWS)):
            cols = slice(g * B_GROUP_DIM, (g + 1) * B_GROUP_DIM)
            y = _dot(pooled_ref[ring, cols], w_grp_ref[g])
            y_ref[ring, cols] = (y * (scale_ref[:, cols] + tie)).astype(jnp.bfloat16)
        return _dot(y_ref[ring, :], w_o_ref[...])

    def finish(k, mixed):
        rows = _sub_rows(k)
        o_ref[rows, :], rstd = _layer_norm_stats(DEEPNORM_ALPHA * x_ref[rows, :] + mixed,
                                                 ln1_g_ref[...], ln1_b_ref[...])
        normed[k] = _zero_after(rstd)

    _software_pipeline(seq // sub, [project, None, mix, finish])


def _mixer_b(x, idx, w_in, w_grp, scale, w_o, ln1_g, ln1_b):
    seq, d = x.shape[1:]
    assert seq == MIX_ROW_TILE
    in_specs = [_row_spec(MIX_ROW_TILE, d), _layer_spec(w_in.shape, 0),
                _layer_spec(w_grp.shape, 0), _layer_spec(scale.shape, 0),
                _layer_spec(w_o.shape, 0), _layer_spec(ln1_g.shape, 1),
                _layer_spec(ln1_b.shape, 1)]
    ring = MIX_RING * MIX_SUB_ROWS
    scratch = [pltpu.VMEM((ring, d), jnp.bfloat16), pltpu.VMEM((ring, d), jnp.bfloat16)]
    return _call(_mixer_b_kernel, "mixer_b", x, MIX_ROW_TILE, in_specs, scratch, idx,
                 (x, w_in, w_grp, scale, w_o, ln1_g, ln1_b))


def _ffn_kernel(idx_ref, prev_ref, x_ref, next_ref, p_ref, w_up_ref, conv_w_ref, conv_b_ref,
                w_down_ref, ln2_g_ref, ln2_b_ref, w_gate_ref, b_gate_ref, w_p_ref, o_ref,
                xs_ref, ys_ref, xb_ref, a_ref):
    rows = FFN_ROW_TILE
    ext = rows + 2 * HALO
    groups = ext // SUBLANES
    slabs = D_MODEL // LANES
    prev, nxt = _halo_rows(prev_ref, next_ref)
    for c in range(slabs):
        lanes = slice(c * LANES, (c + 1) * LANES)
        xs_ref[c, 0:HALO, :] = prev[:, lanes]
        xs_ref[c, HALO:HALO + rows, :] = x_ref[:, lanes]
        xs_ref[c, HALO + rows:, :] = nxt[:, lanes]
    for i in range(0, groups, 2):
        xb_ref[i * SUBLANES:(i + 2) * SUBLANES, :] = jnp.concatenate(
            [jnp.concatenate([xs_ref[c, pl.ds(j, SUBLANES, stride=groups), :]
                              for c in range(slabs)], axis=1)
             for j in (i, i + 1)], axis=0).astype(jnp.bfloat16)

    def up_project(col0):
        return _dot(xb_ref[...], w_up_ref[:, col0:col0 + FF_TILE])

    def conv(h, col0):
        cols = slice(col0, col0 + LANES)
        last = ext - SUBLANES
        below = jnp.concatenate([pltpu.roll(h[last:, :], 1, axis=0), h[:last, :]], axis=0)
        above = jnp.concatenate([h[SUBLANES:, :], pltpu.roll(h[:SUBLANES, :], SUBLANES - 1, axis=0)],
                                axis=0)
        return (conv_w_ref[0:1, cols] * below + conv_w_ref[1:2, cols] * h
                + conv_w_ref[2:3, cols] * above + conv_b_ref[:, cols])

    for j in range(D_FF // FF_TILE):
        hg = up_project(j * FF_TILE)
        hv = up_project(D_FF + j * FF_TILE)
        for c in range(FF_TILE // LANES):
            lanes = slice(c * LANES, (c + 1) * LANES)
            col0 = j * FF_TILE + c * LANES
            gate = conv(hg[:, lanes], col0)
            val = conv(hv[:, lanes], D_FF + col0)
            a_ref[:, col0:col0 + LANES] = (_gelu_x2(gate) * val).astype(jnp.bfloat16)

    edges = [(ext * k // FFN_TAIL_BLOCKS + 15) // 16 * 16 for k in range(FFN_TAIL_BLOCKS)] + [ext]
    for lo, hi in zip(edges[:-1], edges[1:]):
        ffn = _dot(a_ref[lo:hi, :], w_down_ref[...])
        for i in range(lo // SUBLANES, hi // SUBLANES):
            at = slice(i * SUBLANES - lo, (i + 1) * SUBLANES - lo)
            for c in range(slabs):
                ys_ref[c, pl.ds(i, SUBLANES, stride=groups), :] = ffn[at, c * LANES:(c + 1) * LANES]

    emb = _dot(p_ref[...].astype(jnp.bfloat16), w_p_ref[...])
    half = rows // FFN_TAIL_BLOCKS

    def norm_gate(k, _):
        blk = slice(k * half, (k + 1) * half)
        ffn = jnp.concatenate([ys_ref[c, HALO + blk.start:HALO + blk.stop, :]
                               for c in range(slabs)], axis=1)
        x2 = _layer_norm(DEEPNORM_ALPHA * x_ref[blk, :] + ffn, ln2_g_ref[...], ln2_b_ref[...])
        return x2, _dot(x2.astype(jnp.bfloat16), w_gate_ref[...]) + b_gate_ref[...]

    def embed(k, parts):
        x2, pre = parts
        blk = slice(k * half, (k + 1) * half)
        o_ref[blk, :] = x2 + jax.nn.sigmoid(pre) * emb[blk, :]

    _software_pipeline(FFN_TAIL_BLOCKS, [norm_gate, embed])


def _ffn(x, idx, p, w_up, conv_w, conv_b, w_down, ln2_g, ln2_b, w_gate, b_gate, w_p):
    seq, d = x.shape[1:]
    rows = FFN_ROW_TILE
    ext = rows + 2 * HALO
    prev, nxt = _halo_specs(rows, seq)
    p_spec = pl.BlockSpec((None, None, rows, PLE_DIM), lambda b, i, idx: (idx[1], b, i, 0))
    in_specs = [prev, _row_spec(rows, d), nxt, p_spec, _layer_spec(w_up.shape, 1),
                _layer_spec(conv_w.shape, 1), _layer_spec(conv_b.shape, 1),
                _layer_spec(w_down.shape, 1), _layer_spec(ln2_g.shape, 1),
                _layer_spec(ln2_b.shape, 1), _layer_spec(w_gate.shape, 1),
                _layer_spec(b_gate.shape, 1), _layer_spec(w_p.shape, 1)]
    scratch = [pltpu.VMEM((d // LANES, ext, LANES), jnp.float32),
               pltpu.VMEM((d // LANES, ext, LANES), jnp.float32),
               pltpu.VMEM((ext, d), jnp.bfloat16),
               pltpu.VMEM((ext, D_FF), jnp.bfloat16)]
    return _call(_ffn_kernel, "conv_ffn", x, rows, in_specs, scratch, idx,
                 (x, x, x, p, w_up, conv_w, conv_b, w_down, ln2_g, ln2_b, w_gate, b_gate, w_p))


def kernel(x, p, a_w_in, a_ln_g, a_ln_b, a_w_s, a_b_s, a_w_o, b_w_in, b_w_grp, b_scale, b_w_o,
           ffn_w_up, ffn_conv_w, ffn_conv_b, ffn_w_down, ln1_g, ln1_b, ln2_g, ln2_b,
           ple_w_p, ple_w_gate, ple_b_gate):
    bf16 = lambda w: w.astype(jnp.bfloat16)
    rows = lambda v: v[:, None, :]
    a_w_in, a_w_s, a_w_o = bf16(a_w_in), bf16(0.5 * a_w_s), bf16(a_w_o)
    b_w_in, b_w_grp, b_w_o = bf16(b_w_in), bf16(b_w_grp), bf16(b_w_o)
    ffn_w_up, ffn_w_down = bf16(ffn_w_up), bf16(0.5 * ffn_w_down)
    ple_w_gate, ple_w_p = bf16(ple_w_gate), bf16(ple_w_p)
    a_ln_g, a_ln_b, b_scale = rows(a_ln_g), rows(a_ln_b), rows(b_scale)
    ln1_g, ln1_b, ln2_g, ln2_b = rows(ln1_g), rows(ln1_b), rows(ln2_g), rows(ln2_b)
    ffn_conv_b, ple_b_gate = rows(ffn_conv_b), rows(ple_b_gate)
    a_bias = jnp.repeat(jnp.swapaxes(0.5 * a_b_s, 1, 2), A_HEAD_DIM, axis=2)
    for i in range(DEPTH):
        idx = jnp.array([i // 2, i], jnp.int32)
        if i % 2 == 0:
            x = _mixer_a(x, idx, a_w_in, a_ln_g, a_ln_b, a_w_s, a_bias, a_w_o, ln1_g, ln1_b)
        else:
            x = _mixer_b(x, idx, b_w_in, b_w_grp, b_scale, b_w_o, ln1_g, ln1_b)
        x = _ffn(x, idx, p, ffn_w_up, ffn_conv_w, ffn_conv_b, ffn_w_down, ln2_g, ln2_b,
                 ple_w_gate, ple_b_gate, ple_w_p)
    return x
```
